```python
import jax, jax.numpy as jnp
from jax import lax
import numpy as np

D_MODEL = 2048
BATCH = 4
SEQ = 4096
DEPTH = 2
DEC_BATCH = 8
DEC_SEQ = 64
PAST_LEN = 2048

CHUNK = 64
HEAD_DIM = 128
A_HEADS = 8
A_WIDTH = A_HEADS * HEAD_DIM
BAND_CHUNKS = 8
BAND_ROWS = BAND_CHUNKS * CHUNK
MAX_REL = 128
B_HEADS = 8
B_WIDTH = B_HEADS * HEAD_DIM
CONV_W = 4
C_HEADS = 8
C_WIDTH = C_HEADS * HEAD_DIM
ROPE_BASE = 10000.0
EPS = 1e-6
NEG_INF = -1e30
IN_SIZES = (A_WIDTH,) * 4 + (3 * B_WIDTH, B_WIDTH, B_HEADS, B_HEADS) + (C_WIDTH,) * 4 + (D_MODEL,) * 3
N_IN = sum(IN_SIZES)

kernel_name = 'hybrid_streaming_encoder_step'


def rms_norm(x, g):
    xf = x.astype(jnp.float32)
    return xf * lax.rsqrt(jnp.mean(xf * xf, axis=-1, keepdims=True) + EPS) * g.astype(jnp.float32)


def group_norm(x, g):
    xf = x.astype(jnp.float32)
    mu = jnp.mean(xf, axis=-1, keepdims=True)
    var = jnp.mean(jnp.square(xf - mu), axis=-1, keepdims=True)
    return (xf - mu) * lax.rsqrt(var + EPS) * g.astype(jnp.float32)


def l2_norm(x):
    xf = x.astype(jnp.float32)
    return xf * lax.rsqrt(jnp.sum(xf * xf, axis=-1, keepdims=True) + EPS)


def rotary(x, pos):
    half = x.shape[-1] // 2
    inv = ROPE_BASE ** (-jnp.arange(half, dtype=jnp.float32) / half)
    ang = pos.astype(jnp.float32)[:, None] * inv[None, :]
    cos = jnp.cos(ang)[None, :, None, :]
    sin = jnp.sin(ang)[None, :, None, :]
    xf = x.astype(jnp.float32)
    x1, x2 = xf[..., :half], xf[..., half:]
    return jnp.concatenate([x1 * cos - x2 * sin, x1 * sin + x2 * cos], axis=-1)


def causal_conv(x, buf, w):
    t = x.shape[1]
    xp = jnp.concatenate([buf.astype(x.dtype), x], axis=1)
    y = xp[:, 0:t] * w[0]
    for i in range(1, CONV_W):
        y = y + xp[:, i:i + t] * w[i]
    return jax.nn.silu(y), xp[:, t:]


def rel_bias_lookup(rel_bias, dist):
    return jnp.take(rel_bias, jnp.clip(dist, -MAX_REL, MAX_REL) + MAX_REL, axis=1).astype(jnp.float32)


def band_attention_prompt(q, k, v, rel_bias):
    b, s, h, d = q.shape
    nc = s // CHUNK
    nb = BAND_CHUNKS + 1
    pad = jnp.zeros((b, BAND_CHUNKS, CHUNK, h, d), k.dtype)
    kp = jnp.concatenate([pad, k.reshape(b, nc, CHUNK, h, d)], axis=1)
    vp = jnp.concatenate([pad.astype(v.dtype), v.reshape(b, nc, CHUNK, h, d)], axis=1)
    kb = jnp.stack([kp[:, o:o + nc] for o in range(nb)], axis=2).reshape(b, nc, nb * CHUNK, h, d)
    vb = jnp.stack([vp[:, o:o + nc] for o in range(nb)], axis=2).reshape(b, nc, nb * CHUNK, h, d)
    qc = q.reshape(b, nc, CHUNK, h, d)
    scores = jnp.einsum('bcihd,bcrhd->bchir', qc, kb).astype(jnp.float32) * (d ** -0.5)
    i = jnp.arange(CHUNK)[:, None]
    r = jnp.arange(nb * CHUNK)[None, :]
    bias = rel_bias_lookup(rel_bias, BAND_ROWS + i - r)
    key_chunk = jnp.arange(nc)[:, None] - (BAND_CHUNKS - jnp.arange(nb * CHUNK) // CHUNK)[None, :]
    valid = key_chunk >= 0
    scores = jnp.where(valid[None, :, None, None, :], scores + bias[None, None], NEG_INF)
    p = jax.nn.softmax(scores, axis=-1).astype(v.dtype)
    out = jnp.einsum('bchir,bcrhd->bcihd', p, vb)
    return out.reshape(b, s, h, d)


def band_attention_sample(q, k, v, ck, cv, rel_bias):
    p_rows, t = ck.shape[1], q.shape[1]
    k_all = jnp.concatenate([ck.astype(k.dtype), k], axis=1)
    v_all = jnp.concatenate([cv.astype(v.dtype), v], axis=1)
    scores = jnp.einsum('bihd,brhd->bhir', q, k_all).astype(jnp.float32) * (HEAD_DIM ** -0.5)
    i = jnp.arange(t)[:, None]
    r = jnp.arange(p_rows + t)[None, :]
    scores = scores + rel_bias_lookup(rel_bias, p_rows + i - r)[None]
    p = jax.nn.softmax(scores, axis=-1).astype(v.dtype)
    out = jnp.einsum('bhir,brhd->bihd', p, v_all)
    return out, k_all[:, t:], v_all[:, t:]


def _blocks(x, block):
    b, t, h = x.shape[:3]
    xh = jnp.moveaxis(x.astype(jnp.float32), 2, 1)
    return xh.reshape((b, h, t // block, block) + xh.shape[3:])


def _unblocks(o):
    n, b, h, l, d = o.shape
    return jnp.moveaxis(o, 0, 2).reshape(b, h, n * l, d).swapaxes(1, 2)


def gated_delta_chunked(q, k, v, g, beta, s0, block):
    qb = _blocks(q, block) * (HEAD_DIM ** -0.5)
    kb = _blocks(k, block)
    vb = _blocks(v, block)
    gcum = jnp.cumsum(_blocks(g, block), axis=-1)
    bb = _blocks(beta, block)
    idx = jnp.arange(block)
    causal = idx[:, None] >= idx[None, :]
    strict = idx[:, None] > idx[None, :]
    decay = jnp.exp(jnp.where(causal, gcum[..., :, None] - gcum[..., None, :], -jnp.inf))
    k_beta = kb * bb[..., None]
    a_mat = jnp.where(strict, jnp.einsum('bhnid,bhnjd->bhnij', k_beta, kb) * decay, 0.0)
    eye = jnp.eye(block, dtype=jnp.float32)
    rhs = jnp.concatenate([vb * bb[..., None], k_beta * jnp.exp(gcum)[..., None]], axis=-1)
    sol = lax.linalg.triangular_solve(a_mat + eye, rhs, left_side=True, lower=True)
    u, w = sol[..., :HEAD_DIM], sol[..., HEAD_DIM:]
    intra = jnp.einsum('bhnid,bhnjd->bhnij', qb, kb) * decay
    q_dec = qb * jnp.exp(gcum)[..., None]
    k_tail = kb * jnp.exp(gcum[..., -1:] - gcum)[..., None]
    g_blk = jnp.exp(gcum[..., -1])[..., None, None]

    def step(s, xs):
        u_c, w_c, q_c, k_c, a_c, g_c = xs
        v_new = u_c - jnp.einsum('bhld,bhde->bhle', w_c, s)
        o = jnp.einsum('bhld,bhde->bhle', q_c, s) + jnp.einsum('bhij,bhje->bhie', a_c, v_new)
        s = s * g_c + jnp.einsum('bhld,bhle->bhde', k_c, v_new)
        return s, o

    xs = tuple(jnp.moveaxis(a, 2, 0) for a in (u, w, q_dec, k_tail, intra, g_blk))
    s, o = lax.scan(step, s0.astype(jnp.float32), xs)
    return _unblocks(o), s


def retention_chunked(q, k, v, log_gamma, s0, block):
    qb, kb, vb = _blocks(q, block), _blocks(k, block), _blocks(v, block)
    idx = jnp.arange(block, dtype=jnp.float32)
    rel = idx[:, None] - idx[None, :]
    lg = log_gamma[:, None, None]
    dmat = jnp.where(rel >= 0, jnp.exp(lg * jnp.maximum(rel, 0.0)), 0.0)
    intra = jnp.einsum('bhnid,bhnjd->bhnij', qb, kb) * dmat[None, :, None]
    xi = jnp.exp(log_gamma[:, None] * (idx + 1.0))[None, :, :, None]
    zeta = jnp.exp(log_gamma[:, None] * (block - 1.0 - idx))[None, :, :, None]
    g_blk = jnp.exp(log_gamma * block)[None, :, None, None]

    def step(s, xs):
        q_c, k_c, v_c, a_c = xs
        o = jnp.einsum('bhij,bhje->bhie', a_c, v_c) + jnp.einsum('bhld,bhde->bhle', q_c, s) * xi
        s = s * g_blk + jnp.einsum('bhld,bhle->bhde', k_c * zeta, v_c)
        return s, o

    xs = tuple(jnp.moveaxis(a, 2, 0) for a in (qb, kb, vb, intra))
    s, o = lax.scan(step, s0.astype(jnp.float32), xs)
    return _unblocks(o), s


def trunk_layer(x, pos, attn_cache, conv_buf, s_delta, s_ret, block, lw):
    (norm_pre, norm_post, w_in, rel_bias, conv_w, a_log, dt_bias, delta_norm, ret_norm,
     w_ba, w_bb, w_bc, w_out) = lw
    b, t, _ = x.shape
    dtype = x.dtype
    h = rms_norm(x, norm_pre).astype(dtype)
    proj = h @ w_in
    cuts = [int(c) for c in np.cumsum(IN_SIZES)[:-1]]
    (aq, ak, av, az, bqkv, bz, bbeta, balpha, cq, ck, cv, cz, ga, gb, gc) = jnp.split(proj, cuts, axis=-1)
    heads = lambda u: u.reshape(b, t, -1, HEAD_DIM)

    qa, ka, va = heads(aq), heads(ak), heads(av)
    if attn_cache is None:
        oa = band_attention_prompt(qa, ka, va, rel_bias)
        keep = min(BAND_ROWS, t)
        new_k, new_v = ka[:, t - keep:], va[:, t - keep:]
    else:
        oa, new_k, new_v = band_attention_sample(qa, ka, va, attn_cache[0], attn_cache[1], rel_bias)
    oa = (oa.reshape(b, t, A_WIDTH) * jax.nn.silu(az)).astype(dtype)

    qkv, new_conv = causal_conv(bqkv, conv_buf, conv_w)
    qb, kb, vb = jnp.split(qkv, 3, axis=-1)
    beta = jax.nn.sigmoid(balpha.astype(jnp.float32) * 0.0 + bbeta.astype(jnp.float32)) if False else jax.nn.sigmoid(bbeta.astype(jnp.float32))
    g = -jnp.exp(a_log.astype(jnp.float32)) * jax.nn.softplus(balpha.astype(jnp.float32) + dt_bias.astype(jnp.float32))
    ob, new_sd = gated_delta_chunked(l2_norm(heads(qb)), l2_norm(heads(kb)), heads(vb), g, beta, s_delta, block)
    ob = (rms_norm(ob, delta_norm).reshape(b, t, B_WIDTH) * jax.nn.silu(bz)).astype(dtype)

    log_gamma = jnp.log1p(-jnp.exp2(-5.0 - jnp.arange(C_HEADS, dtype=jnp.float32)))
    qc = rotary(heads(cq), pos)
    kc = rotary(heads(ck), pos) * (HEAD_DIM ** -0.5)
    oc, new_sr = retention_chunked(qc, kc, heads(cv), log_gamma, s_ret, block)
    oc = (group_norm(oc, ret_norm).reshape(b, t, C_WIDTH) * jax.nn.silu(cz)).astype(dtype)

    merged = (jax.nn.sigmoid(ga) * (oa @ w_ba) + jax.nn.sigmoid(gb) * (ob @ w_bb)
              + jax.nn.sigmoid(gc) * (oc @ w_bc))
    y = rms_norm(merged @ w_out, norm_post).astype(dtype)
    return (x + y, new_k, new_v, new_conv, new_sd.astype(s_delta.dtype), new_sr.astype(s_ret.dtype))


def setup_inputs(seed: int = 0) -> dict:
    key = jax.random.key(seed)
    ks = jax.random.split(key, 24)
    f32 = jnp.float32
    nrm = lambda k, shape, scale: scale * jax.random.normal(k, shape, f32)
    p_cache = min(BAND_ROWS, PAST_LEN)
    dt_init = jnp.exp(jax.random.uniform(ks[13], (DEPTH, B_HEADS), f32, float(np.log(1e-3)), float(np.log(1e-1))))
    return {
        'x_prompt': nrm(ks[0], (BATCH, SEQ, D_MODEL), 1.0),
        'x_sample': nrm(ks[1], (DEC_BATCH, DEC_SEQ, D_MODEL), 1.0),
        'cache_attn_k': nrm(ks[2], (DEPTH, DEC_BATCH, p_cache, A_HEADS, HEAD_DIM), 1.0),
        'cache_attn_v': nrm(ks[3], (DEPTH, DEC_BATCH, p_cache, A_HEADS, HEAD_DIM), 1.0),
        'state_conv': nrm(ks[4], (DEPTH, DEC_BATCH, CONV_W - 1, 3 * B_WIDTH), 1.0),
        'state_delta': nrm(ks[5], (DEPTH, DEC_BATCH, B_HEADS, HEAD_DIM, HEAD_DIM), 0.1),
        'state_ret': nrm(ks[6], (DEPTH, DEC_BATCH, C_HEADS, HEAD_DIM, HEAD_DIM), 0.3),
        'norm_pre': 1.0 + nrm(ks[7], (DEPTH, D_MODEL), 0.02),
        'norm_post': 1.0 + nrm(ks[8], (DEPTH, D_MODEL), 0.02),
        'w_in': nrm(ks[9], (DEPTH, D_MODEL, N_IN), D_MODEL ** -0.5),
        'attn_rel_bias': nrm(ks[10], (DEPTH, A_HEADS, 2 * MAX_REL + 1), 0.3),
        'conv_w': nrm(ks[11], (DEPTH, CONV_W, 3 * B_WIDTH), CONV_W ** -0.5),
        'delta_a_log': jnp.log(jax.random.uniform(ks[12], (DEPTH, B_HEADS), f32, 1.0, 16.0)),
        'delta_dt_bias': jnp.log(jnp.expm1(dt_init)),
        'delta_norm': 1.0 + nrm(ks[14], (DEPTH, HEAD_DIM), 0.02),
        'ret_norm': 1.0 + nrm(ks[15], (DEPTH, C_HEADS, HEAD_DIM), 0.02),
        'w_branch_a': nrm(ks[16], (DEPTH, A_WIDTH, D_MODEL), A_WIDTH ** -0.5),
        'w_branch_b': nrm(ks[17], (DEPTH, B_WIDTH, D_MODEL), B_WIDTH ** -0.5),
        'w_branch_c': nrm(ks[18], (DEPTH, C_WIDTH, D_MODEL), C_WIDTH ** -0.5),
        'w_out': nrm(ks[19], (DEPTH, D_MODEL, D_MODEL), D_MODEL ** -0.5),
    }


def reference(x_prompt, x_sample, cache_attn_k, cache_attn_v, state_conv, state_delta, state_ret,
              norm_pre, norm_post, w_in, attn_rel_bias, conv_w, delta_a_log, delta_dt_bias,
              delta_norm, ret_norm, w_branch_a, w_branch_b, w_branch_c, w_out):
    dtype = x_prompt.dtype
    bp, tp = x_prompt.shape[0], x_prompt.shape[1]
    ts = x_sample.shape[1]
    pos_p = jnp.arange(tp, dtype=jnp.int32)
    pos_s = PAST_LEN + jnp.arange(ts, dtype=jnp.int32)
    zero_conv = jnp.zeros((bp, CONV_W - 1, 3 * B_WIDTH), dtype)
    zero_delta = jnp.zeros((bp, B_HEADS, HEAD_DIM, HEAD_DIM), dtype)
    zero_ret = jnp.zeros((bp, C_HEADS, HEAD_DIM, HEAD_DIM), dtype)
    xp, xs = x_prompt, x_sample
    acc_p = [[], [], [], [], []]
    acc_s = [[], [], [], [], []]
    for l in range(DEPTH):
        lw = (norm_pre[l], norm_post[l], w_in[l], attn_rel_bias[l], conv_w[l], delta_a_log[l],
              delta_dt_bias[l], delta_norm[l], ret_norm[l], w_branch_a[l], w_branch_b[l],
              w_branch_c[l], w_out[l])
        xp, *st_p = trunk_layer(xp, pos_p, None, zero_conv, zero_delta, zero_ret, CHUNK, lw)
        xs, *st_s = trunk_layer(xs, pos_s, (cache_attn_k[l], cache_attn_v[l]), state_conv[l],
                                state_delta[l], state_ret[l], ts, lw)
        for acc, val in zip(acc_p, st_p):
            acc.append(val)
        for acc, val in zip(acc_s, st_s):
            acc.append(val)
    pk, pv, pc, pd, pr = [jnp.stack(a, axis=0) for a in acc_p]
    sk, sv, sc, sd, sr = [jnp.stack(a, axis=0) for a in acc_s]
    return (xp, xs, pk, pv, pc, pd, pr, sk, sv, sc, sd, sr)
```

```python
import functools

import jax
import jax.numpy as jnp
import numpy as np
from jax import lax
from jax.experimental import pallas as pl
from jax.experimental.pallas import tpu as pltpu

F32 = jnp.float32
BF16 = jnp.bfloat16

D_MODEL = 2048
CHUNK = 64
HEAD_DIM = 128
HEADS = 8
WIDTH = HEADS * HEAD_DIM
BAND_CHUNKS = 8
BAND_ROWS = BAND_CHUNKS * CHUNK
BAND_KEYS = BAND_ROWS + CHUNK
MAX_REL = 128
CONV_W = 4
PAST_LEN = 2048
ROPE_BASE = 10000.0
EPS = 1e-6
NEG_INF = -1e30
SCALE = HEAD_DIM ** -0.5

N_MAIN = 18 * WIDTH
GATE_COL0 = 8 * WIDTH
GATE_LANES = 128
V7X_VMEM_LIMIT = 56 * 1024 * 1024

_NT = (((1,), (1,)), ((), ()))
_TN = (((0,), (0,)), ((), ()))


def _dot(a, b):
    return jnp.dot(a, b, preferred_element_type=F32)


def _dot_nt(a, b):
    return lax.dot_general(a, b, _NT, preferred_element_type=F32)


def _dot_tn(a, b):
    return lax.dot_general(a, b, _TN, preferred_element_type=F32)


def _silu(x):
    return x * jax.nn.sigmoid(x)


def _softplus(x):
    return jnp.maximum(x, 0.0) + jnp.log1p(jnp.exp(-jnp.abs(x)))


def _inproj_kernel(x_ref, g_ref, w_ref, wg_ref, o_ref, og_ref, h_ref):
    @pl.when(pl.program_id(1) == 0)
    def _():
        x = x_ref[...]
        ms = jnp.mean(x * x, axis=-1, keepdims=True)
        h = (x * lax.rsqrt(ms + EPS) * g_ref[...]).astype(BF16)
        h_ref[...] = h
        og_ref[...] = _dot(h, wg_ref[...])

    o_ref[...] = _dot(h_ref[...], w_ref[...])


def _inproj(x2d, g, w_main, w_gate):
    m = x2d.shape[0]
    tm = min(m, 1024)
    tn = 1024
    return pl.pallas_call(
        _inproj_kernel,
        grid=(m // tm, N_MAIN // tn),
        in_specs=[
            pl.BlockSpec((tm, D_MODEL), lambda i, n: (i, 0)),
            pl.BlockSpec((1, D_MODEL), lambda i, n: (0, 0)),
            pl.BlockSpec((D_MODEL, tn), lambda i, n: (0, n)),
            pl.BlockSpec((D_MODEL, GATE_LANES), lambda i, n: (0, 0)),
        ],
        out_specs=[
            pl.BlockSpec((tm, tn), lambda i, n: (i, n)),
            pl.BlockSpec((tm, GATE_LANES), lambda i, n: (i, 0)),
        ],
        out_shape=[
            jax.ShapeDtypeStruct((m, N_MAIN), F32),
            jax.ShapeDtypeStruct((m, GATE_LANES), F32),
        ],
        scratch_shapes=[pltpu.VMEM((tm, D_MODEL), BF16)],
        compiler_params=pltpu.CompilerParams(
            dimension_semantics=("arbitrary", "arbitrary"),
            vmem_limit_bytes=V7X_VMEM_LIMIT),
        name="inproj",
    )(x2d, g, w_main, w_gate)


def _attn_kernel(q_ref, kp_ref, ko_ref, vp_ref, vo_ref, z_ref, bias_ref, o_ref,
                 kcat, vcat, *, nq, has_cache):
    t = pl.program_id(1)
    rows = nq * CHUNK
    kcat[0:BAND_ROWS, :] = kp_ref[...].astype(BF16)
    kcat[BAND_ROWS:BAND_ROWS + rows, :] = ko_ref[...].astype(BF16)
    vcat[0:BAND_ROWS, :] = vp_ref[...].astype(BF16)
    vcat[BAND_ROWS:BAND_ROWS + rows, :] = vo_ref[...].astype(BF16)
    bias = bias_ref[...]
    col = lax.broadcasted_iota(jnp.int32, (CHUNK, BAND_KEYS), 1)
    for j in range(nq):
        r0 = j * CHUNK
        q = q_ref[r0:r0 + CHUNK, :].astype(BF16)
        s = _dot_nt(q, kcat[r0:r0 + BAND_KEYS, :]) * SCALE + bias
        if not has_cache:
            first_valid = jnp.where(t == 0, BAND_ROWS - r0, 0)
            s = jnp.where(col < first_valid, NEG_INF, s)
        m = jnp.max(s, axis=-1, keepdims=True)
        p = jnp.exp(s - m)
        l = jnp.sum(p, axis=-1, keepdims=True)
        o = _dot(p.astype(BF16), vcat[r0:r0 + BAND_KEYS, :]) / l
        o_ref[r0:r0 + CHUNK, :] = (o * _silu(z_ref[r0:r0 + CHUNK, :])).astype(BF16)


def _attn(p3, bias, cache):
    b, t, _ = p3.shape
    has_cache = cache is not None
    rows = CHUNK if has_cache else BAND_ROWS
    assert t % rows == 0 and (not has_cache or t == CHUNK)
    nq = rows // CHUNK
    cur = lambda blk: pl.BlockSpec((None, rows, HEAD_DIM), lambda i, tt, h: (i, tt, blk * HEADS + h))
    if has_cache:
        prev = lambda blk: pl.BlockSpec((None, BAND_ROWS, HEAD_DIM), lambda i, tt, h: (i, 0, h))
        kprev, vprev = cache
    else:
        prev = lambda blk: pl.BlockSpec(
            (None, BAND_ROWS, HEAD_DIM), lambda i, tt, h: (i, jnp.maximum(tt - 1, 0), blk * HEADS + h))
        kprev, vprev = p3, p3
    return pl.pallas_call(
        functools.partial(_attn_kernel, nq=nq, has_cache=has_cache),
        grid=(b, t // rows, HEADS),
        in_specs=[cur(0), prev(1), cur(1), prev(2), cur(2), cur(3),
                  pl.BlockSpec((None, CHUNK, BAND_KEYS), lambda i, tt, h: (h, 0, 0))],
        out_specs=pl.BlockSpec((None, rows, HEAD_DIM), lambda i, tt, h: (i, tt, h)),
        out_shape=jax.ShapeDtypeStruct((b, t, WIDTH), BF16),
        scratch_shapes=[pltpu.VMEM((BAND_ROWS + rows, HEAD_DIM), BF16),
                        pltpu.VMEM((BAND_ROWS + rows, HEAD_DIM), BF16)],
        compiler_params=pltpu.CompilerParams(
            dimension_semantics=("arbitrary", "arbitrary", "arbitrary")),
        name="attn",
    )(p3, kprev, p3, vprev, p3, p3, bias)


def _delta_kernel(*refs, has_state):
    if has_state:
        (q_ref, k_ref, v_ref, z_ref, gt_ref, cw_ref, al_ref, dt_ref, dn_ref, c0_ref, s0_ref,
         o_ref, sout_ref, xbuf, st) = refs
    else:
        (q_ref, k_ref, v_ref, z_ref, gt_ref, cw_ref, al_ref, dt_ref, dn_ref,
         o_ref, sout_ref, xbuf, st) = refs
    c = pl.program_id(1)
    nc = pl.num_programs(1)

    @pl.when(c == 0)
    def _():
        if has_state:
            for p in range(3):
                xbuf[p, 0:8, :] = c0_ref[:, p * WIDTH:(p + 1) * WIDTH]
            st[...] = s0_ref[...]
        else:
            xbuf[:, 0:8, :] = jnp.zeros((3, 8, WIDTH), F32)
            st[...] = jnp.zeros_like(st)

    @pl.when(c > 0)
    def _():
        xbuf[:, 0:8, :] = xbuf[:, CHUNK:CHUNK + 8, :]

    xbuf[0, 8:8 + CHUNK, :] = q_ref[...]
    xbuf[1, 8:8 + CHUNK, :] = k_ref[...]
    xbuf[2, 8:8 + CHUNK, :] = v_ref[...]

    acts = []
    for p in range(3):
        y = None
        for i in range(CONV_W):
            start = 8 - (CONV_W - 1) + i
            term = xbuf[p, start:start + CHUNK, :] * cw_ref[i:i + 1, p * WIDTH:(p + 1) * WIDTH]
            y = term if y is None else y + term
        acts.append(_silu(y))
    qa, ka, va = acts

    gt = gt_ref[...]
    beta_all = jax.nn.sigmoid(gt)
    g_all = -jnp.exp(al_ref[...]) * _softplus(gt + dt_ref[...])
    ri = lax.broadcasted_iota(jnp.int32, (CHUNK, CHUNK), 0)
    ci = lax.broadcasted_iota(jnp.int32, (CHUNK, CHUNK), 1)
    causal = ri >= ci
    strict = ri > ci
    gcum = jnp.dot(causal.astype(F32), g_all, precision=lax.Precision.HIGHEST,
                   preferred_element_type=F32)
    gcum_t = jnp.transpose(gcum)
    eg_all = jnp.exp(gcum)
    glast = gcum[CHUNK - 1:CHUNK, :]
    tail_all = jnp.exp(glast - gcum)
    gblk_all = jnp.exp(glast)
    dn = dn_ref[...]

    for h in range(HEADS):
        sl = slice(h * HEAD_DIM, (h + 1) * HEAD_DIM)
        qh, kh, vh = qa[:, sl], ka[:, sl], va[:, sl]
        qh = qh * lax.rsqrt(jnp.sum(qh * qh, axis=-1, keepdims=True) + EPS)
        kh = kh * lax.rsqrt(jnp.sum(kh * kh, axis=-1, keepdims=True) + EPS)
        beta = beta_all[:, h:h + 1]
        gl = HEADS + h
        eg = eg_all[:, gl:gl + 1]
        tail = tail_all[:, gl:gl + 1]
        gblk = gblk_all[:, gl:gl + 1]
        diff = gcum[:, gl:gl + 1] - gcum_t[gl:gl + 1, :]
        decay = jnp.exp(jnp.where(causal, diff, -jnp.inf))
        kbeta = kh * beta
        k16 = kh.astype(BF16)
        a = jnp.where(strict, _dot_nt(kbeta.astype(BF16), k16) * decay, 0.0)
        x = jnp.concatenate([vh * beta, kbeta * eg], axis=1)
        nm = -a
        for i in range(6):
            nm16 = nm.astype(BF16)
            x = x + _dot(nm16, x.astype(BF16))
            if i < 5:
                nm = _dot(nm16, nm16)
        u, w = x[:, :HEAD_DIM], x[:, HEAD_DIM:]
        qs = qh * SCALE
        intra = _dot_nt(qs.astype(BF16), k16) * decay
        s_h = st[h]
        s16 = s_h.astype(BF16)
        vnew = u - _dot(w.astype(BF16), s16)
        vnew16 = vnew.astype(BF16)
        o = _dot((qs * eg).astype(BF16), s16) + _dot(intra.astype(BF16), vnew16)
        st[h] = s_h * gblk + _dot_tn((kh * tail).astype(BF16), vnew16)
        o = o * lax.rsqrt(jnp.mean(o * o, axis=-1, keepdims=True) + EPS) * dn
        o_ref[:, sl] = (o * _silu(z_ref[:, sl])).astype(BF16)

    @pl.when(c == nc - 1)
    def _():
        sout_ref[...] = st[...]


def _delta(p3, gt3, conv_w, al_row, dt_row, dn_row, conv0, s0):
    b, t, _ = p3.shape
    has_state = s0 is not None
    col = lambda blk: pl.BlockSpec((None, CHUNK, WIDTH), lambda i, c: (i, c, blk))
    full = lambda shape: pl.BlockSpec(shape, lambda i, c: (0,) * len(shape))
    in_specs = [col(4), col(5), col(6), col(7),
                pl.BlockSpec((None, CHUNK, GATE_LANES), lambda i, c: (i, c, 0)),
                full((CONV_W, 3 * WIDTH)), full((1, GATE_LANES)), full((1, GATE_LANES)),
                full((1, HEAD_DIM))]
    args = [p3, p3, p3, p3, gt3, conv_w, al_row, dt_row, dn_row]
    if has_state:
        in_specs += [pl.BlockSpec((None, 8, 3 * WIDTH), lambda i, c: (i, 0, 0)),
                     pl.BlockSpec((None, HEADS, HEAD_DIM, HEAD_DIM), lambda i, c: (i, 0, 0, 0))]
        args += [conv0, s0]
    return pl.pallas_call(
        functools.partial(_delta_kernel, has_state=has_state),
        grid=(b, t // CHUNK),
        in_specs=in_specs,
        out_specs=[pl.BlockSpec((None, CHUNK, WIDTH), lambda i, c: (i, c, 0)),
                   pl.BlockSpec((None, HEADS, HEAD_DIM, HEAD_DIM), lambda i, c: (i, 0, 0, 0))],
        out_shape=[jax.ShapeDtypeStruct((b, t, WIDTH), BF16),
                   jax.ShapeDtypeStruct((b, HEADS, HEAD_DIM, HEAD_DIM), F32)],
        scratch_shapes=[pltpu.VMEM((3, CHUNK + 8, WIDTH), F32),
                        pltpu.VMEM((HEADS, HEAD_DIM, HEAD_DIM), F32)],
        compiler_params=pltpu.CompilerParams(dimension_semantics=("arbitrary", "arbitrary")),
        name="delta",
    )(*args)


def _ret_kernel(*refs, has_state):
    if has_state:
        (q_ref, k_ref, v_ref, z_ref, cc_ref, ss_ref, dm_ref, xi_ref, zeta_ref, gb_ref, rn_ref,
         s0_ref, o_ref, sout_ref, st) = refs
    else:
        (q_ref, k_ref, v_ref, z_ref, cc_ref, ss_ref, dm_ref, xi_ref, zeta_ref, gb_ref, rn_ref,
         o_ref, sout_ref, st) = refs
    c = pl.program_id(1)
    nc = pl.num_programs(1)

    @pl.when(c == 0)
    def _():
        if has_state:
            st[...] = s0_ref[...]
        else:
            st[...] = jnp.zeros_like(st)

    cc = cc_ref[...]
    ss = ss_ref[...]

    def rot(x):
        return x * cc + pltpu.roll(x, HEAD_DIM // 2, axis=1) * ss

    for h in range(HEADS):
        sl = slice(h * HEAD_DIM, (h + 1) * HEAD_DIM)
        qh = rot(q_ref[:, sl])
        kh = rot(k_ref[:, sl]) * SCALE
        q16 = qh.astype(BF16)
        v16 = v_ref[:, sl].astype(BF16)
        intra = _dot_nt(q16, kh.astype(BF16)) * dm_ref[h]
        s_h = st[h]
        o = _dot(intra.astype(BF16), v16) + _dot(q16, s_h.astype(BF16)) * xi_ref[h]
        st[h] = s_h * gb_ref[h] + _dot_tn((kh * zeta_ref[h]).astype(BF16), v16)
        mu = jnp.mean(o, axis=-1, keepdims=True)
        d = o - mu
        var = jnp.mean(d * d, axis=-1, keepdims=True)
        o = d * lax.rsqrt(var + EPS) * rn_ref[:, sl]
        o_ref[:, sl] = (o * _silu(z_ref[:, sl])).astype(BF16)

    @pl.when(c == nc - 1)
    def _():
        sout_ref[...] = st[...]


def _ret(p3, cc, ss, consts, rn_row, s0):
    b, t, _ = p3.shape
    has_state = s0 is not None
    dmat, xi, zeta, gblk = consts
    col = lambda blk: pl.BlockSpec((None, CHUNK, WIDTH), lambda i, c: (i, c, blk))
    full = lambda shape: pl.BlockSpec(shape, lambda i, c: (0,) * len(shape))
    tab = pl.BlockSpec((CHUNK, HEAD_DIM), lambda i, c: (c, 0))
    in_specs = [col(8), col(9), col(10), col(11), tab, tab,
                full((HEADS, CHUNK, CHUNK)), full((HEADS, CHUNK, HEAD_DIM)),
                full((HEADS, CHUNK, HEAD_DIM)), full((HEADS, 1, HEAD_DIM)), full((1, WIDTH))]
    args = [p3, p3, p3, p3, cc, ss, dmat, xi, zeta, gblk, rn_row]
    if has_state:
        in_specs.append(pl.BlockSpec((None, HEADS, HEAD_DIM, HEAD_DIM), lambda i, c: (i, 0, 0, 0)))
        args.append(s0)
    return pl.pallas_call(
        functools.partial(_ret_kernel, has_state=has_state),
        grid=(b, t // CHUNK),
        in_specs=in_specs,
        out_specs=[pl.BlockSpec((None, CHUNK, WIDTH), lambda i, c: (i, c, 0)),
                   pl.BlockSpec((None, HEADS, HEAD_DIM, HEAD_DIM), lambda i, c: (i, 0, 0, 0))],
        out_shape=[jax.ShapeDtypeStruct((b, t, WIDTH), BF16),
                   jax.ShapeDtypeStruct((b, HEADS, HEAD_DIM, HEAD_DIM), F32)],
        scratch_shapes=[pltpu.VMEM((HEADS, HEAD_DIM, HEAD_DIM), F32)],
        compiler_params=pltpu.CompilerParams(dimension_semantics=("arbitrary", "arbitrary")),
        name="ret",
    )(*args)


def _merge_kernel(oa_ref, ob_ref, oc_ref, ga_ref, gb_ref, gc_ref, wa_ref, wb_ref, wc_ref,
                  wo_ref, x_ref, g_ref, o_ref):
    merged = (jax.nn.sigmoid(ga_ref[...]) * _dot(oa_ref[...], wa_ref[...])
              + jax.nn.sigmoid(gb_ref[...]) * _dot(ob_ref[...], wb_ref[...])
              + jax.nn.sigmoid(gc_ref[...]) * _dot(oc_ref[...], wc_ref[...]))
    y = _dot(merged.astype(BF16), wo_ref[...])
    y = y * lax.rsqrt(jnp.mean(y * y, axis=-1, keepdims=True) + EPS) * g_ref[...]
    o_ref[...] = x_ref[...] + y


def _merge(oa, ob, oc, p2, x2d, wa, wb, wc, wo, g):
    m = x2d.shape[0]
    tm = min(m, 256)
    row = lambda w: pl.BlockSpec((tm, w), lambda i: (i, 0))
    gate = lambda blk: pl.BlockSpec((tm, D_MODEL), lambda i: (i, blk))
    const = lambda shape: pl.BlockSpec(shape, lambda i: (0, 0), pipeline_mode=pl.Buffered(1))
    return pl.pallas_call(
        _merge_kernel,
        grid=(m // tm,),
        in_specs=[row(WIDTH), row(WIDTH), row(WIDTH), gate(6), gate(7), gate(8),
                  const((WIDTH, D_MODEL)), const((WIDTH, D_MODEL)), const((WIDTH, D_MODEL)),
                  const((D_MODEL, D_MODEL)), row(D_MODEL), const((1, D_MODEL))],
        out_specs=row(D_MODEL),
        out_shape=jax.ShapeDtypeStruct((m, D_MODEL), F32),
        compiler_params=pltpu.CompilerParams(
            dimension_semantics=("arbitrary",), vmem_limit_bytes=V7X_VMEM_LIMIT),
        name="merge",
    )(oa, ob, oc, p2, p2, p2, wa, wb, wc, wo, x2d, g)


def _retention_constants():
    log_gamma = jnp.log1p(-jnp.exp2(-5.0 - jnp.arange(HEADS, dtype=F32)))
    idx = jnp.arange(CHUNK, dtype=F32)
    rel = idx[:, None] - idx[None, :]
    lg = log_gamma[:, None, None]
    dmat = jnp.where(rel >= 0, jnp.exp(lg * jnp.maximum(rel, 0.0)), 0.0)
    xi = jnp.exp(log_gamma[:, None] * (idx + 1.0))
    zeta = jnp.exp(log_gamma[:, None] * (CHUNK - 1.0 - idx))
    gblk = jnp.exp(log_gamma * CHUNK)
    bc = lambda v: jnp.broadcast_to(v[:, :, None], (HEADS, v.shape[1], HEAD_DIM))
    return dmat, bc(xi), bc(zeta), bc(gblk[:, None])


def _rotary_tables(pos):
    half = HEAD_DIM // 2
    inv = ROPE_BASE ** (-jnp.arange(half, dtype=F32) / half)
    ang = pos.astype(F32)[:, None] * inv[None, :]
    cos, sin = jnp.cos(ang), jnp.sin(ang)
    return jnp.concatenate([cos, cos], axis=1), jnp.concatenate([-sin, sin], axis=1)


def _band_bias(rel_bias):
    i = jnp.arange(CHUNK)[:, None]
    r = jnp.arange(BAND_KEYS)[None, :]
    idx = jnp.clip(BAND_ROWS + i - r, -MAX_REL, MAX_REL) + MAX_REL
    return jnp.take(rel_bias, idx, axis=1).astype(F32)


def _lane_row(v, lane0):
    return jnp.zeros((1, GATE_LANES), F32).at[0, lane0:lane0 + v.shape[0]].set(v.astype(F32))


def _group_layer(x, pos, cache, conv_buf, s_delta, s_ret, lw, ret_consts):
    (norm_pre, norm_post, w_main, w_gate, bias, conv_w, al_row, dt_row, dn_row, rn_row,
     wa, wb, wc, wo) = lw
    b, t, _ = x.shape
    x2d = x.reshape(b * t, D_MODEL)
    p2, gt2 = _inproj(x2d, norm_pre, w_main, w_gate)
    p3 = p2.reshape(b, t, N_MAIN)
    gt3 = gt2.reshape(b, t, GATE_LANES)

    if cache is None:
        oa = _attn(p3, bias, None)
        keep = min(BAND_ROWS, t)
        new_k = p3[:, t - keep:, WIDTH:2 * WIDTH].reshape(b, keep, HEADS, HEAD_DIM)
        new_v = p3[:, t - keep:, 2 * WIDTH:3 * WIDTH].reshape(b, keep, HEADS, HEAD_DIM)
    else:
        ck, cv = cache
        rows = ck.shape[1]
        oa = _attn(p3, bias, (ck.reshape(b, rows, WIDTH), cv.reshape(b, rows, WIDTH)))
        k_new = p3[:, :, WIDTH:2 * WIDTH].reshape(b, t, HEADS, HEAD_DIM)
        v_new = p3[:, :, 2 * WIDTH:3 * WIDTH].reshape(b, t, HEADS, HEAD_DIM)
        new_k = jnp.concatenate([ck, k_new], axis=1)[:, t:]
        new_v = jnp.concatenate([cv, v_new], axis=1)[:, t:]

    if conv_buf is None:
        conv0 = None
        new_conv = p3[:, t - (CONV_W - 1):, 4 * WIDTH:7 * WIDTH]
    else:
        conv0 = jnp.pad(conv_buf, ((0, 0), (8 - (CONV_W - 1), 0), (0, 0)))
        new_conv = jnp.concatenate([conv_buf, p3[:, :, 4 * WIDTH:7 * WIDTH]], axis=1)[:, t:]
    ob, new_sd = _delta(p3, gt3, conv_w, al_row, dt_row, dn_row, conv0, s_delta)

    cc, ss = _rotary_tables(pos)
    oc, new_sr = _ret(p3, cc, ss, ret_consts, rn_row, s_ret)

    y2d = _merge(oa.reshape(b * t, WIDTH), ob.reshape(b * t, WIDTH), oc.reshape(b * t, WIDTH),
                 p2, x2d, wa, wb, wc, wo, norm_post)
    return y2d.reshape(b, t, D_MODEL), new_k, new_v, new_conv, new_sd, new_sr


def kernel(x_prompt, x_sample, cache_attn_k, cache_attn_v, state_conv, state_delta, state_ret,
           norm_pre, norm_post, w_in, attn_rel_bias, conv_w, delta_a_log, delta_dt_bias,
           delta_norm, ret_norm, w_branch_a, w_branch_b, w_branch_c, w_out):
    depth = w_in.shape[0]
    tp, ts = x_prompt.shape[1], x_sample.shape[1]
    pos_p = jnp.arange(tp, dtype=jnp.int32)
    pos_s = PAST_LEN + jnp.arange(ts, dtype=jnp.int32)
    ret_consts = _retention_constants()
    xp, xs = x_prompt, x_sample
    acc_p = [[] for _ in range(5)]
    acc_s = [[] for _ in range(5)]
    for l in range(depth):
        w = w_in[l]
        w_main = jnp.concatenate([w[:, :GATE_COL0], w[:, GATE_COL0 + 2 * HEADS:]], axis=1).astype(BF16)
        w_gate = jnp.pad(w[:, GATE_COL0:GATE_COL0 + 2 * HEADS],
                         ((0, 0), (0, GATE_LANES - 2 * HEADS))).astype(BF16)
        lw = (norm_pre[l][None, :], norm_post[l][None, :], w_main, w_gate,
              _band_bias(attn_rel_bias[l]), conv_w[l],
              _lane_row(delta_a_log[l], HEADS), _lane_row(delta_dt_bias[l], HEADS),
              delta_norm[l][None, :].astype(F32), ret_norm[l].reshape(1, WIDTH).astype(F32),
              w_branch_a[l].astype(BF16), w_branch_b[l].astype(BF16), w_branch_c[l].astype(BF16),
              w_out[l].astype(BF16))
        xp, *st_p = _group_layer(xp, pos_p, None, None, None, None, lw, ret_consts)
        xs, *st_s = _group_layer(xs, pos_s, (cache_attn_k[l], cache_attn_v[l]), state_conv[l],
                                 state_delta[l], state_ret[l], lw, ret_consts)
        for acc, val in zip(acc_p, st_p):
            acc.append(val)
        for acc, val in zip(acc_s, st_s):
            acc.append(val)
    outs_p = [jnp.stack(a, axis=0) for a in acc_p]
    outs_s = [jnp.stack(a, axis=0) for a in acc_s]
    return (xp, xs, *outs_p, *outs_s)
```

```python
import functools

import jax
import jax.numpy as jnp
import numpy as np
from jax import lax
from jax.experimental import pallas as pl
from jax.experimental.pallas import tpu as pltpu

F32 = jnp.float32
BF16 = jnp.bfloat16

D_MODEL = 2048
CHUNK = 64
HEAD_DIM = 128
HEADS = 8
WIDTH = HEADS * HEAD_DIM
BAND_CHUNKS = 8
BAND_ROWS = BAND_CHUNKS * CHUNK
BAND_KEYS = BAND_ROWS + CHUNK
MAX_REL = 128
CONV_W = 4
PAST_LEN = 2048
ROPE_BASE = 10000.0
EPS = 1e-6
NEG_INF = -1e30
SCALE = HEAD_DIM ** -0.5

N_MAIN = 18 * WIDTH
GATE_COL0 = 8 * WIDTH
GATE_LANES = 128
V7X_VMEM_LIMIT = 56 * 1024 * 1024
MIXER_CHUNKS_PER_STEP = 4

_NT = (((1,), (1,)), ((), ()))
_TN = (((0,), (0,)), ((), ()))


def _dot(a, b):
    return jnp.dot(a, b, preferred_element_type=F32)


def _dot_nt(a, b):
    return lax.dot_general(a, b, _NT, preferred_element_type=F32)


def _dot_tn(a, b):
    return lax.dot_general(a, b, _TN, preferred_element_type=F32)


def _silu(x):
    return x * jax.nn.sigmoid(x)


def _softplus(x):
    return jnp.maximum(x, 0.0) + jnp.log1p(jnp.exp(-jnp.abs(x)))


def _inproj_kernel(x_ref, g_ref, w_ref, wg_ref, o_ref, og_ref, h_ref):
    @pl.when(pl.program_id(1) == 0)
    def _():
        x = x_ref[...]
        ms = jnp.mean(x * x, axis=-1, keepdims=True)
        h = (x * lax.rsqrt(ms + EPS) * g_ref[...]).astype(BF16)
        h_ref[...] = h
        og_ref[...] = _dot(h, wg_ref[...])

    o_ref[...] = _dot(h_ref[...], w_ref[...])


def _inproj(x2d, g, w_main, w_gate):
    m = x2d.shape[0]
    tm = min(m, 1024)
    tn = 1024
    return pl.pallas_call(
        _inproj_kernel,
        grid=(m // tm, N_MAIN // tn),
        in_specs=[
            pl.BlockSpec((tm, D_MODEL), lambda i, n: (i, 0)),
            pl.BlockSpec((1, D_MODEL), lambda i, n: (0, 0)),
            pl.BlockSpec((D_MODEL, tn), lambda i, n: (0, n)),
            pl.BlockSpec((D_MODEL, GATE_LANES), lambda i, n: (0, 0)),
        ],
        out_specs=[
            pl.BlockSpec((tm, tn), lambda i, n: (i, n)),
            pl.BlockSpec((tm, GATE_LANES), lambda i, n: (i, 0)),
        ],
        out_shape=[
            jax.ShapeDtypeStruct((m, N_MAIN), F32),
            jax.ShapeDtypeStruct((m, GATE_LANES), F32),
        ],
        scratch_shapes=[pltpu.VMEM((tm, D_MODEL), BF16)],
        compiler_params=pltpu.CompilerParams(
            dimension_semantics=("arbitrary", "arbitrary"),
            vmem_limit_bytes=V7X_VMEM_LIMIT),
        name="inproj",
    )(x2d, g, w_main, w_gate)


def _attn_kernel(q_ref, kp_ref, ko_ref, vp_ref, vo_ref, z_ref, bias_ref, o_ref,
                 kcat, vcat, *, nq, has_cache):
    t = pl.program_id(1)
    rows = nq * CHUNK
    kcat[0:BAND_ROWS, :] = kp_ref[...].astype(BF16)
    kcat[BAND_ROWS:BAND_ROWS + rows, :] = ko_ref[...].astype(BF16)
    vcat[0:BAND_ROWS, :] = vp_ref[...].astype(BF16)
    vcat[BAND_ROWS:BAND_ROWS + rows, :] = vo_ref[...].astype(BF16)
    bias = bias_ref[...]
    col = lax.broadcasted_iota(jnp.int32, (CHUNK, BAND_KEYS), 1)
    scores = []
    for j in range(nq):
        r0 = j * CHUNK
        q = q_ref[r0:r0 + CHUNK, :].astype(BF16)
        scores.append(_dot_nt(q, kcat[r0:r0 + BAND_KEYS, :]))
    probs = []
    for j in range(nq):
        s = scores[j] * SCALE + bias
        if not has_cache:
            first_valid = jnp.where(t == 0, BAND_ROWS - j * CHUNK, 0)
            s = jnp.where(col < first_valid, NEG_INF, s)
        m = jnp.max(s, axis=-1, keepdims=True)
        p = jnp.exp(s - m)
        probs.append((p.astype(BF16), jnp.sum(p, axis=-1, keepdims=True)))
    outs = []
    for j in range(nq):
        r0 = j * CHUNK
        outs.append(_dot(probs[j][0], vcat[r0:r0 + BAND_KEYS, :]))
    for j in range(nq):
        r0 = j * CHUNK
        o = outs[j] / probs[j][1]
        o_ref[r0:r0 + CHUNK, :] = (o * _silu(z_ref[r0:r0 + CHUNK, :])).astype(BF16)


def _attn(p3, bias, cache):
    b, t, _ = p3.shape
    has_cache = cache is not None
    rows = CHUNK if has_cache else BAND_ROWS
    assert t % rows == 0 and (not has_cache or t == CHUNK)
    nq = rows // CHUNK
    cur = lambda blk: pl.BlockSpec((None, rows, HEAD_DIM), lambda i, tt, h: (i, tt, blk * HEADS + h))
    if has_cache:
        prev = lambda blk: pl.BlockSpec((None, BAND_ROWS, HEAD_DIM), lambda i, tt, h: (i, 0, h))
        kprev, vprev = cache
    else:
        prev = lambda blk: pl.BlockSpec(
            (None, BAND_ROWS, HEAD_DIM), lambda i, tt, h: (i, jnp.maximum(tt - 1, 0), blk * HEADS + h))
        kprev, vprev = p3, p3
    return pl.pallas_call(
        functools.partial(_attn_kernel, nq=nq, has_cache=has_cache),
        grid=(b, t // rows, HEADS),
        in_specs=[cur(0), prev(1), cur(1), prev(2), cur(2), cur(3),
                  pl.BlockSpec((None, CHUNK, BAND_KEYS), lambda i, tt, h: (h, 0, 0))],
        out_specs=pl.BlockSpec((None, rows, HEAD_DIM), lambda i, tt, h: (i, tt, h)),
        out_shape=jax.ShapeDtypeStruct((b, t, WIDTH), BF16),
        scratch_shapes=[pltpu.VMEM((BAND_ROWS + rows, HEAD_DIM), BF16),
                        pltpu.VMEM((BAND_ROWS + rows, HEAD_DIM), BF16)],
        compiler_params=pltpu.CompilerParams(
            dimension_semantics=("arbitrary", "arbitrary", "arbitrary")),
        name="attn",
    )(p3, kprev, p3, vprev, p3, p3, bias)


def _delta_kernel(*refs, has_state, nch):
    if has_state:
        (q_ref, k_ref, v_ref, z_ref, gt_ref, cw_ref, al_ref, dt_ref, dn_ref, c0_ref, s0_ref,
         o_ref, sout_ref, xbuf, act, st) = refs
    else:
        (q_ref, k_ref, v_ref, z_ref, gt_ref, cw_ref, al_ref, dt_ref, dn_ref,
         o_ref, sout_ref, xbuf, act, st) = refs
    c = pl.program_id(1)
    nc = pl.num_programs(1)
    rows = nch * CHUNK

    @pl.when(c == 0)
    def _():
        if has_state:
            for p in range(3):
                xbuf[p, 0:8, :] = c0_ref[:, p * WIDTH:(p + 1) * WIDTH]
            st[...] = s0_ref[...]
        else:
            xbuf[:, 0:8, :] = jnp.zeros((3, 8, WIDTH), F32)
            st[...] = jnp.zeros_like(st)

    @pl.when(c > 0)
    def _():
        xbuf[:, 0:8, :] = xbuf[:, rows:rows + 8, :]

    xbuf[0, 8:8 + rows, :] = q_ref[...]
    xbuf[1, 8:8 + rows, :] = k_ref[...]
    xbuf[2, 8:8 + rows, :] = v_ref[...]

    for p in range(3):
        y = None
        for i in range(CONV_W):
            start = 8 - (CONV_W - 1) + i
            term = xbuf[p, start:start + rows, :] * cw_ref[i:i + 1, p * WIDTH:(p + 1) * WIDTH]
            y = term if y is None else y + term
        act[p] = _silu(y)

    ri = lax.broadcasted_iota(jnp.int32, (CHUNK, CHUNK), 0)
    ci = lax.broadcasted_iota(jnp.int32, (CHUNK, CHUNK), 1)
    causal = ri >= ci
    strict = ri > ci
    ltri = causal.astype(F32)
    dn = dn_ref[...]

    chains = []
    for cc in range(nch):
        rs = slice(cc * CHUNK, (cc + 1) * CHUNK)
        gt = gt_ref[rs, :]
        beta_all = jax.nn.sigmoid(gt)
        g_all = -jnp.exp(al_ref[...]) * _softplus(gt + dt_ref[...])
        gcum = jnp.dot(ltri, g_all, precision=lax.Precision.HIGHEST,
                       preferred_element_type=F32)
        gcum_t = jnp.transpose(gcum)
        eg_all = jnp.exp(gcum)
        glast = gcum[CHUNK - 1:CHUNK, :]
        tail_all = jnp.exp(glast - gcum)
        gblk_all = jnp.exp(glast)
        for h in range(HEADS):
            sl = slice(h * HEAD_DIM, (h + 1) * HEAD_DIM)
            qh, kh, vh = act[0, rs, sl], act[1, rs, sl], act[2, rs, sl]
            qh = qh * lax.rsqrt(jnp.sum(qh * qh, axis=-1, keepdims=True) + EPS)
            kh = kh * lax.rsqrt(jnp.sum(kh * kh, axis=-1, keepdims=True) + EPS)
            beta = beta_all[:, h:h + 1]
            gl = HEADS + h
            eg = eg_all[:, gl:gl + 1]
            diff = gcum[:, gl:gl + 1] - gcum_t[gl:gl + 1, :]
            kbeta = kh * beta
            qs = qh * SCALE
            chains.append(dict(
                rs=rs, sl=sl, h=h,
                decay=jnp.exp(jnp.where(causal, diff, -jnp.inf)),
                k16=kh.astype(BF16), kbeta16=kbeta.astype(BF16), qs16=qs.astype(BF16),
                qdec=qs * eg,
                ktail16=(kh * tail_all[:, gl:gl + 1]).astype(BF16),
                gblk=gblk_all[:, gl:gl + 1],
                x=jnp.concatenate([vh * beta, kbeta * eg], axis=1)))

    for ch in chains:
        ch["a"] = jnp.where(strict, _dot_nt(ch["kbeta16"], ch["k16"]) * ch["decay"], 0.0)
    for ch in chains:
        ch["intra16"] = (_dot_nt(ch["qs16"], ch["k16"]) * ch["decay"]).astype(BF16)

    def same_block(size):
        shift = size.bit_length() - 1
        return jnp.right_shift(ri, shift) == jnp.right_shift(ci, shift)

    eye = (ri == ci).astype(F32)
    for ch in chains:
        n1 = jnp.where(same_block(8), -ch["a"], 0.0)
        ch["n1_16"] = n1.astype(BF16)
        ch["p"] = eye + n1
    for ch in chains:
        ch["n2"] = _dot(ch["n1_16"], ch["n1_16"])
    for ch in chains:
        ch["n2_16"] = ch["n2"].astype(BF16)
        ch["n4"] = _dot(ch["n2_16"], ch["n2_16"])
        ch["p"] = ch["p"] + _dot(ch["p"].astype(BF16), ch["n2_16"])
    for ch in chains:
        ch["t"] = ch["p"] + _dot(ch["p"].astype(BF16), ch["n4"].astype(BF16))
    for size in (8, 16, 32):
        couple = jnp.logical_and(same_block(2 * size), jnp.logical_not(same_block(size)))
        for ch in chains:
            ch["t16"] = ch["t"].astype(BF16)
            ch["te"] = _dot(ch["t16"], jnp.where(couple, ch["a"], 0.0).astype(BF16))
        for ch in chains:
            ch["t"] = ch["t"] - _dot(ch["te"].astype(BF16), ch["t16"])
    for ch in chains:
        ch["x"] = _dot(ch["t"].astype(BF16), ch["x"].astype(BF16))

    for ch in chains:
        x16 = ch["x"].astype(BF16)
        ix = _dot(ch["intra16"], x16)
        kx = _dot_tn(ch["ktail16"], x16)
        ch["o_intra"] = ix[:, :HEAD_DIM]
        ch["qeff16"] = (ch["qdec"] - ix[:, HEAD_DIM:]).astype(BF16)
        ch["b"] = kx[:, :HEAD_DIM]
        ch["p16"] = kx[:, HEAD_DIM:].astype(BF16)

    state = [st[h] for h in range(HEADS)]
    for cc in range(nch):
        for ch in chains[cc * HEADS:(cc + 1) * HEADS]:
            h, rs, sl = ch["h"], ch["rs"], ch["sl"]
            s16 = state[h].astype(BF16)
            o = _dot(ch["qeff16"], s16) + ch["o_intra"]
            state[h] = state[h] * ch["gblk"] - _dot(ch["p16"], s16) + ch["b"]
            o = o * lax.rsqrt(jnp.mean(o * o, axis=-1, keepdims=True) + EPS) * dn
            o_ref[rs, sl] = (o * _silu(z_ref[rs, sl])).astype(BF16)
    for h in range(HEADS):
        st[h] = state[h]

    @pl.when(c == nc - 1)
    def _():
        sout_ref[...] = st[...]


def _delta(p3, gt3, conv_w, al_row, dt_row, dn_row, conv0, s0):
    b, t, _ = p3.shape
    has_state = s0 is not None
    nch = min(MIXER_CHUNKS_PER_STEP, t // CHUNK)
    rows = nch * CHUNK
    assert t % rows == 0
    col = lambda blk: pl.BlockSpec((None, rows, WIDTH), lambda i, c: (i, c, blk))
    full = lambda shape: pl.BlockSpec(shape, lambda i, c: (0,) * len(shape))
    in_specs = [col(4), col(5), col(6), col(7),
                pl.BlockSpec((None, rows, GATE_LANES), lambda i, c: (i, c, 0)),
                full((CONV_W, 3 * WIDTH)), full((1, GATE_LANES)), full((1, GATE_LANES)),
                full((1, HEAD_DIM))]
    args = [p3, p3, p3, p3, gt3, conv_w, al_row, dt_row, dn_row]
    if has_state:
        in_specs += [pl.BlockSpec((None, 8, 3 * WIDTH), lambda i, c: (i, 0, 0)),
                     pl.BlockSpec((None, HEADS, HEAD_DIM, HEAD_DIM), lambda i, c: (i, 0, 0, 0))]
        args += [conv0, s0]
    return pl.pallas_call(
        functools.partial(_delta_kernel, has_state=has_state, nch=nch),
        grid=(b, t // rows),
        in_specs=in_specs,
        out_specs=[pl.BlockSpec((None, rows, WIDTH), lambda i, c: (i, c, 0)),
                   pl.BlockSpec((None, HEADS, HEAD_DIM, HEAD_DIM), lambda i, c: (i, 0, 0, 0))],
        out_shape=[jax.ShapeDtypeStruct((b, t, WIDTH), BF16),
                   jax.ShapeDtypeStruct((b, HEADS, HEAD_DIM, HEAD_DIM), F32)],
        scratch_shapes=[pltpu.VMEM((3, rows + 8, WIDTH), F32),
                        pltpu.VMEM((3, rows, WIDTH), F32),
                        pltpu.VMEM((HEADS, HEAD_DIM, HEAD_DIM), F32)],
        compiler_params=pltpu.CompilerParams(
            dimension_semantics=("arbitrary", "arbitrary"), vmem_limit_bytes=V7X_VMEM_LIMIT),
        name="delta",
    )(*args)


def _ret_kernel(*refs, has_state, nch):
    if has_state:
        (q_ref, k_ref, v_ref, z_ref, cc_ref, ss_ref, dm_ref, xi_ref, zeta_ref, gb_ref, rn_ref,
         s0_ref, o_ref, sout_ref, st) = refs
    else:
        (q_ref, k_ref, v_ref, z_ref, cc_ref, ss_ref, dm_ref, xi_ref, zeta_ref, gb_ref, rn_ref,
         o_ref, sout_ref, st) = refs
    c = pl.program_id(1)
    nc = pl.num_programs(1)

    @pl.when(c == 0)
    def _():
        if has_state:
            st[...] = s0_ref[...]
        else:
            st[...] = jnp.zeros_like(st)

    chains = []
    for cc in range(nch):
        rs = slice(cc * CHUNK, (cc + 1) * CHUNK)
        cos2 = cc_ref[rs, :]
        sin2 = ss_ref[rs, :]
        for h in range(HEADS):
            sl = slice(h * HEAD_DIM, (h + 1) * HEAD_DIM)
            q = q_ref[rs, sl]
            k = k_ref[rs, sl]
            qh = q * cos2 + pltpu.roll(q, HEAD_DIM // 2, axis=1) * sin2
            kh = (k * cos2 + pltpu.roll(k, HEAD_DIM // 2, axis=1) * sin2) * SCALE
            chains.append(dict(rs=rs, sl=sl, h=h, q16=qh.astype(BF16), k16=kh.astype(BF16),
                               kz16=(kh * zeta_ref[h]).astype(BF16),
                               v16=v_ref[rs, sl].astype(BF16)))
    for ch in chains:
        ch["intra16"] = (_dot_nt(ch["q16"], ch["k16"]) * dm_ref[ch["h"]]).astype(BF16)
    for ch in chains:
        ch["kv"] = _dot_tn(ch["kz16"], ch["v16"])
    for ch in chains:
        ch["o_intra"] = _dot(ch["intra16"], ch["v16"])

    state = [st[h] for h in range(HEADS)]
    for cc in range(nch):
        for ch in chains[cc * HEADS:(cc + 1) * HEADS]:
            h, rs, sl = ch["h"], ch["rs"], ch["sl"]
            o = ch["o_intra"] + _dot(ch["q16"], state[h].astype(BF16)) * xi_ref[h]
            state[h] = state[h] * gb_ref[h] + ch["kv"]
            mu = jnp.mean(o, axis=-1, keepdims=True)
            d = o - mu
            var = jnp.mean(d * d, axis=-1, keepdims=True)
            o = d * lax.rsqrt(var + EPS) * rn_ref[:, sl]
            o_ref[rs, sl] = (o * _silu(z_ref[rs, sl])).astype(BF16)
    for h in range(HEADS):
        st[h] = state[h]

    @pl.when(c == nc - 1)
    def _():
        sout_ref[...] = st[...]


def _ret(p3, cc, ss, consts, rn_row, s0):
    b, t, _ = p3.shape
    has_state = s0 is not None
    nch = min(MIXER_CHUNKS_PER_STEP, t // CHUNK)
    rows = nch * CHUNK
    assert t % rows == 0
    dmat, xi, zeta, gblk = consts
    col = lambda blk: pl.BlockSpec((None, rows, WIDTH), lambda i, c: (i, c, blk))
    full = lambda shape: pl.BlockSpec(shape, lambda i, c: (0,) * len(shape))
    tab = pl.BlockSpec((rows, HEAD_DIM), lambda i, c: (c, 0))
    in_specs = [col(8), col(9), col(10), col(11), tab, tab,
                full((HEADS, CHUNK, CHUNK)), full((HEADS, CHUNK, HEAD_DIM)),
                full((HEADS, CHUNK, HEAD_DIM)), full((HEADS, 1, HEAD_DIM)), full((1, WIDTH))]
    args = [p3, p3, p3, p3, cc, ss, dmat, xi, zeta, gblk, rn_row]
    if has_state:
        in_specs.append(pl.BlockSpec((None, HEADS, HEAD_DIM, HEAD_DIM), lambda i, c: (i, 0, 0, 0)))
        args.append(s0)
    return pl.pallas_call(
        functools.partial(_ret_kernel, has_state=has_state, nch=nch),
        grid=(b, t // rows),
        in_specs=in_specs,
        out_specs=[pl.BlockSpec((None, rows, WIDTH), lambda i, c: (i, c, 0)),
                   pl.BlockSpec((None, HEADS, HEAD_DIM, HEAD_DIM), lambda i, c: (i, 0, 0, 0))],
        out_shape=[jax.ShapeDtypeStruct((b, t, WIDTH), BF16),
                   jax.ShapeDtypeStruct((b, HEADS, HEAD_DIM, HEAD_DIM), F32)],
        scratch_shapes=[pltpu.VMEM((HEADS, HEAD_DIM, HEAD_DIM), F32)],
        compiler_params=pltpu.CompilerParams(
            dimension_semantics=("arbitrary", "arbitrary"), vmem_limit_bytes=V7X_VMEM_LIMIT),
        name="ret",
    )(*args)


def _merge_kernel(oa_ref, ob_ref, oc_ref, ga_ref, gb_ref, gc_ref, wa_ref, wb_ref, wc_ref,
                  wo_ref, x_ref, g_ref, o_ref):
    merged = (jax.nn.sigmoid(ga_ref[...]) * _dot(oa_ref[...], wa_ref[...])
              + jax.nn.sigmoid(gb_ref[...]) * _dot(ob_ref[...], wb_ref[...])
              + jax.nn.sigmoid(gc_ref[...]) * _dot(oc_ref[...], wc_ref[...]))
    y = _dot(merged.astype(BF16), wo_ref[...])
    y = y * lax.rsqrt(jnp.mean(y * y, axis=-1, keepdims=True) + EPS) * g_ref[...]
    o_ref[...] = x_ref[...] + y


def _merge(oa, ob, oc, p2, x2d, wa, wb, wc, wo, g):
    m = x2d.shape[0]
    tm = min(m, 256)
    row = lambda w: pl.BlockSpec((tm, w), lambda i: (i, 0))
    gate = lambda blk: pl.BlockSpec((tm, D_MODEL), lambda i: (i, blk))
    const = lambda shape: pl.BlockSpec(shape, lambda i: (0, 0), pipeline_mode=pl.Buffered(1))
    return pl.pallas_call(
        _merge_kernel,
        grid=(m // tm,),
        in_specs=[row(WIDTH), row(WIDTH), row(WIDTH), gate(6), gate(7), gate(8),
                  const((WIDTH, D_MODEL)), const((WIDTH, D_MODEL)), const((WIDTH, D_MODEL)),
                  const((D_MODEL, D_MODEL)), row(D_MODEL), const((1, D_MODEL))],
        out_specs=row(D_MODEL),
        out_shape=jax.ShapeDtypeStruct((m, D_MODEL), F32),
        compiler_params=pltpu.CompilerParams(
            dimension_semantics=("arbitrary",), vmem_limit_bytes=V7X_VMEM_LIMIT),
        name="merge",
    )(oa, ob, oc, p2, p2, p2, wa, wb, wc, wo, x2d, g)


def _retention_constants():
    log_gamma = jnp.log1p(-jnp.exp2(-5.0 - jnp.arange(HEADS, dtype=F32)))
    idx = jnp.arange(CHUNK, dtype=F32)
    rel = idx[:, None] - idx[None, :]
    lg = log_gamma[:, None, None]
    dmat = jnp.where(rel >= 0, jnp.exp(lg * jnp.maximum(rel, 0.0)), 0.0)
    xi = jnp.exp(log_gamma[:, None] * (idx + 1.0))
    zeta = jnp.exp(log_gamma[:, None] * (CHUNK - 1.0 - idx))
    gblk = jnp.exp(log_gamma * CHUNK)
    bc = lambda v: jnp.broadcast_to(v[:, :, None], (HEADS, v.shape[1], HEAD_DIM))
    return dmat, bc(xi), bc(zeta), bc(gblk[:, None])


def _rotary_tables(pos):
    half = HEAD_DIM // 2
    inv = ROPE_BASE ** (-jnp.arange(half, dtype=F32) / half)
    ang = pos.astype(F32)[:, None] * inv[None, :]
    cos, sin = jnp.cos(ang), jnp.sin(ang)
    return jnp.concatenate([cos, cos], axis=1), jnp.concatenate([-sin, sin], axis=1)


def _band_bias(rel_bias):
    i = jnp.arange(CHUNK)[:, None]
    r = jnp.arange(BAND_KEYS)[None, :]
    idx = jnp.clip(BAND_ROWS + i - r, -MAX_REL, MAX_REL) + MAX_REL
    return jnp.take(rel_bias, idx, axis=1).astype(F32)


def _lane_row(v, lane0):
    return jnp.zeros((1, GATE_LANES), F32).at[0, lane0:lane0 + v.shape[0]].set(v.astype(F32))


def _group_layer(x, pos, cache, conv_buf, s_delta, s_ret, lw, ret_consts):
    (norm_pre, norm_post, w_main, w_gate, bias, conv_w, al_row, dt_row, dn_row, rn_row,
     wa, wb, wc, wo) = lw
    b, t, _ = x.shape
    x2d = x.reshape(b * t, D_MODEL)
    p2, gt2 = _inproj(x2d, norm_pre, w_main, w_gate)
    p3 = p2.reshape(b, t, N_MAIN)
    gt3 = gt2.reshape(b, t, GATE_LANES)

    if cache is None:
        oa = _attn(p3, bias, None)
        keep = min(BAND_ROWS, t)
        new_k = p3[:, t - keep:, WIDTH:2 * WIDTH].reshape(b, keep, HEADS, HEAD_DIM)
        new_v = p3[:, t - keep:, 2 * WIDTH:3 * WIDTH].reshape(b, keep, HEADS, HEAD_DIM)
    else:
        ck, cv = cache
        rows = ck.shape[1]
        oa = _attn(p3, bias, (ck.reshape(b, rows, WIDTH), cv.reshape(b, rows, WIDTH)))
        k_new = p3[:, :, WIDTH:2 * WIDTH].reshape(b, t, HEADS, HEAD_DIM)
        v_new = p3[:, :, 2 * WIDTH:3 * WIDTH].reshape(b, t, HEADS, HEAD_DIM)
        new_k = jnp.concatenate([ck, k_new], axis=1)[:, t:]
        new_v = jnp.concatenate([cv, v_new], axis=1)[:, t:]

    if conv_buf is None:
        conv0 = None
        new_conv = p3[:, t - (CONV_W - 1):, 4 * WIDTH:7 * WIDTH]
    else:
        conv0 = jnp.pad(conv_buf, ((0, 0), (8 - (CONV_W - 1), 0), (0, 0)))
        new_conv = jnp.concatenate([conv_buf, p3[:, :, 4 * WIDTH:7 * WIDTH]], axis=1)[:, t:]
    ob, new_sd = _delta(p3, gt3, conv_w, al_row, dt_row, dn_row, conv0, s_delta)

    cc, ss = _rotary_tables(pos)
    oc, new_sr = _ret(p3, cc, ss, ret_consts, rn_row, s_ret)

    y2d = _merge(oa.reshape(b * t, WIDTH), ob.reshape(b * t, WIDTH), oc.reshape(b * t, WIDTH),
                 p2, x2d, wa, wb, wc, wo, norm_post)
    return y2d.reshape(b, t, D_MODEL), new_k, new_v, new_conv, new_sd, new_sr


def kernel(x_prompt, x_sample, cache_attn_k, cache_attn_v, state_conv, state_delta, state_ret,
           norm_pre, norm_post, w_in, attn_rel_bias, conv_w, delta_a_log, delta_dt_bias,
           delta_norm, ret_norm, w_branch_a, w_branch_b, w_branch_c, w_out):
    depth = w_in.shape[0]
    tp, ts = x_prompt.shape[1], x_sample.shape[1]
    pos_p = jnp.arange(tp, dtype=jnp.int32)
    pos_s = PAST_LEN + jnp.arange(ts, dtype=jnp.int32)
    ret_consts = _retention_constants()
    xp, xs = x_prompt, x_sample
    acc_p = [[] for _ in range(5)]
    acc_s = [[] for _ in range(5)]
    for l in range(depth):
        w = w_in[l]
        w_main = jnp.concatenate([w[:, :GATE_COL0], w[:, GATE_COL0 + 2 * HEADS:]], axis=1).astype(BF16)
        w_gate = jnp.pad(w[:, GATE_COL0:GATE_COL0 + 2 * HEADS],
                         ((0, 0), (0, GATE_LANES - 2 * HEADS))).astype(BF16)
        lw = (norm_pre[l][None, :], norm_post[l][None, :], w_main, w_gate,
              _band_bias(attn_rel_bias[l]), conv_w[l],
              _lane_row(delta_a_log[l], HEADS), _lane_row(delta_dt_bias[l], HEADS),
              delta_norm[l][None, :].astype(F32), ret_norm[l].reshape(1, WIDTH).astype(F32),
              w_branch_a[l].astype(BF16), w_branch_b[l].astype(BF16), w_branch_c[l].astype(BF16),
              w_out[l].astype(BF16))
        xp, *st_p = _group_layer(xp, pos_p, None, None, None, None, lw, ret_consts)
        xs, *st_s = _group_layer(xs, pos_s, (cache_attn_k[l], cache_attn_v[l]), state_conv[l],
                                 state_delta[l], state_ret[l], lw, ret_consts)
        for acc, val in zip(acc_p, st_p):
            acc.append(val)
        for acc, val in zip(acc_s, st_s):
            acc.append(val)
    outs_p = [jnp.stack(a, axis=0) for a in acc_p]
    outs_s = [jnp.stack(a, axis=0) for a in acc_s]
    return (xp, xs, *outs_p, *outs_s)
```

```python
import functools

import jax
import jax.numpy as jnp
import numpy as np
from jax import lax
from jax.experimental import pallas as pl
from jax.experimental.pallas import tpu as pltpu

F32 = jnp.float32
BF16 = jnp.bfloat16

D_MODEL = 2048
CHUNK = 64
HEAD_DIM = 128
HEADS = 8
WIDTH = HEADS * HEAD_DIM
BAND_CHUNKS = 8
BAND_ROWS = BAND_CHUNKS * CHUNK
BAND_KEYS = BAND_ROWS + CHUNK
BIAS_LANES = 640
MAX_REL = 128
CONV_W = 4
PAST_LEN = 2048
ROPE_BASE = 10000.0
EPS = 1e-6
NEG_INF = -1e30
SCALE = HEAD_DIM ** -0.5

N_MAIN = 18 * WIDTH
GATE_COL0 = 8 * WIDTH
GATE_LANES = 128
V7X_VMEM_LIMIT = 56 * 1024 * 1024
MIXER_CHUNKS_PER_STEP = 4

_NT = (((1,), (1,)), ((), ()))
_TN = (((0,), (0,)), ((), ()))


def _dot(a, b):
    return jnp.dot(a, b, preferred_element_type=F32)


def _dot_nt(a, b):
    return lax.dot_general(a, b, _NT, preferred_element_type=F32)


def _dot_tn(a, b):
    return lax.dot_general(a, b, _TN, preferred_element_type=F32)


def _silu(x):
    return x * jax.nn.sigmoid(x)


def _softplus(x):
    return jnp.maximum(x, 0.0) + jnp.log1p(jnp.exp(-jnp.abs(x)))


def _inproj_kernel(x_ref, g_ref, w_ref, wg_ref, o_ref, og_ref, h_ref):
    @pl.when(pl.program_id(1) == 0)
    def _():
        x = x_ref[...]
        ms = jnp.mean(x * x, axis=-1, keepdims=True)
        h = (x * lax.rsqrt(ms + EPS) * g_ref[...]).astype(BF16)
        h_ref[...] = h
        og_ref[...] = _dot(h, wg_ref[...])

    o_ref[...] = _dot(h_ref[...], w_ref[...])


def _inproj(x2d, g, w_main, w_gate):
    m = x2d.shape[0]
    tm = min(m, 1024)
    tn = 1024
    return pl.pallas_call(
        _inproj_kernel,
        grid=(m // tm, N_MAIN // tn),
        in_specs=[
            pl.BlockSpec((tm, D_MODEL), lambda i, n: (i, 0)),
            pl.BlockSpec((1, D_MODEL), lambda i, n: (0, 0)),
            pl.BlockSpec((D_MODEL, tn), lambda i, n: (0, n)),
            pl.BlockSpec((D_MODEL, GATE_LANES), lambda i, n: (0, 0)),
        ],
        out_specs=[
            pl.BlockSpec((tm, tn), lambda i, n: (i, n)),
            pl.BlockSpec((tm, GATE_LANES), lambda i, n: (i, 0)),
        ],
        out_shape=[
            jax.ShapeDtypeStruct((m, N_MAIN), F32),
            jax.ShapeDtypeStruct((m, GATE_LANES), F32),
        ],
        scratch_shapes=[pltpu.VMEM((tm, D_MODEL), BF16)],
        compiler_params=pltpu.CompilerParams(
            dimension_semantics=("arbitrary", "arbitrary"),
            vmem_limit_bytes=V7X_VMEM_LIMIT),
        name="inproj",
    )(x2d, g, w_main, w_gate)


def _attn_kernel(*refs, nq, has_cache):
    if has_cache:
        (q_ref, ko_ref, vo_ref, z_ref, rb_ref, kc_ref, vc_ref,
         o_ref, nk_ref, nv_ref, kcat, vcat, bias_scr) = refs
    else:
        (q_ref, ko_ref, vo_ref, z_ref, rb_ref,
         o_ref, nk_ref, nv_ref, kcat, vcat, bias_scr) = refs
    t = pl.program_id(1)
    nt = pl.num_programs(1)
    rows = nq * CHUNK

    @pl.when(jnp.logical_and(pl.program_id(0) == 0, t == 0))
    def _():
        for h in range(HEADS):
            wide = jnp.broadcast_to(rb_ref[h:h + 1, :], (CHUNK, BIAS_LANES))
            bias_scr[h] = pltpu.roll(wide, BIAS_LANES - (CHUNK - 1), 1, stride=1, stride_axis=0)

    if has_cache:
        kcat[0:BAND_ROWS, :] = kc_ref[...].astype(BF16)
        vcat[0:BAND_ROWS, :] = vc_ref[...].astype(BF16)
    else:
        @pl.when(t == 0)
        def _():
            kcat[0:BAND_ROWS, :] = jnp.zeros((BAND_ROWS, WIDTH), BF16)
            vcat[0:BAND_ROWS, :] = jnp.zeros((BAND_ROWS, WIDTH), BF16)

        @pl.when(t > 0)
        def _():
            kcat[0:BAND_ROWS, :] = kcat[rows:rows + BAND_ROWS, :]
            vcat[0:BAND_ROWS, :] = vcat[rows:rows + BAND_ROWS, :]
    kcat[BAND_ROWS:BAND_ROWS + rows, :] = ko_ref[...].astype(BF16)
    vcat[BAND_ROWS:BAND_ROWS + rows, :] = vo_ref[...].astype(BF16)

    col = lax.broadcasted_iota(jnp.int32, (CHUNK, BAND_KEYS), 1)
    for h in range(HEADS):
        sl = slice(h * HEAD_DIM, (h + 1) * HEAD_DIM)
        bias = bias_scr[h][:, :BAND_KEYS]
        scores = []
        for j in range(nq):
            r0 = j * CHUNK
            q = q_ref[r0:r0 + CHUNK, sl].astype(BF16)
            scores.append(_dot_nt(q, kcat[r0:r0 + BAND_KEYS, sl]))
        probs = []
        for j in range(nq):
            s = scores[j] * SCALE + bias
            if not has_cache:
                first_valid = jnp.where(t == 0, BAND_ROWS - j * CHUNK, 0)
                s = jnp.where(col < first_valid, NEG_INF, s)
            m = jnp.max(s, axis=-1, keepdims=True)
            p = jnp.exp(s - m)
            probs.append((p.astype(BF16), jnp.sum(p, axis=-1, keepdims=True)))
        outs = []
        for j in range(nq):
            r0 = j * CHUNK
            outs.append(_dot(probs[j][0], vcat[r0:r0 + BAND_KEYS, sl]))
        for j in range(nq):
            r0 = j * CHUNK
            o = outs[j] / probs[j][1]
            o_ref[r0:r0 + CHUNK, sl] = (o * _silu(z_ref[r0:r0 + CHUNK, sl])).astype(BF16)

    @pl.when(t == nt - 1)
    def _():
        if rows < BAND_ROWS:
            nk_ref[0:BAND_ROWS - rows, :] = kc_ref[rows:BAND_ROWS, :]
            nv_ref[0:BAND_ROWS - rows, :] = vc_ref[rows:BAND_ROWS, :]
        nk_ref[BAND_ROWS - rows:BAND_ROWS, :] = ko_ref[...]
        nv_ref[BAND_ROWS - rows:BAND_ROWS, :] = vo_ref[...]


def _attn(p3, rev_bias, cache):
    b, t, _ = p3.shape
    has_cache = cache is not None
    rows = CHUNK if has_cache else BAND_ROWS
    assert t % rows == 0 and (not has_cache or t == CHUNK)
    nq = rows // CHUNK
    cur = lambda blk: pl.BlockSpec((None, rows, WIDTH), lambda i, tt: (i, tt, blk))
    whole = pl.BlockSpec((None, BAND_ROWS, WIDTH), lambda i, tt: (i, 0, 0))
    in_specs = [cur(0), cur(1), cur(2), cur(3),
                pl.BlockSpec((HEADS, BIAS_LANES), lambda i, tt: (0, 0))]
    args = [p3, p3, p3, p3, rev_bias]
    if has_cache:
        in_specs += [whole, whole]
        args += list(cache)
    return pl.pallas_call(
        functools.partial(_attn_kernel, nq=nq, has_cache=has_cache),
        grid=(b, t // rows),
        in_specs=in_specs,
        out_specs=[cur(0), whole, whole],
        out_shape=[jax.ShapeDtypeStruct((b, t, WIDTH), BF16),
                   jax.ShapeDtypeStruct((b, BAND_ROWS, WIDTH), F32),
                   jax.ShapeDtypeStruct((b, BAND_ROWS, WIDTH), F32)],
        scratch_shapes=[pltpu.VMEM((BAND_ROWS + rows, WIDTH), BF16),
                        pltpu.VMEM((BAND_ROWS + rows, WIDTH), BF16),
                        pltpu.VMEM((HEADS, CHUNK, BIAS_LANES), F32)],
        compiler_params=pltpu.CompilerParams(
            dimension_semantics=("arbitrary", "arbitrary"), vmem_limit_bytes=V7X_VMEM_LIMIT),
        name="attn",
    )(*args)


def _delta_kernel(*refs, has_state, nch):
    if has_state:
        (q_ref, k_ref, v_ref, z_ref, gt_ref, cw_ref, al_ref, dt_ref, dn_ref, c0_ref, s0_ref,
         o_ref, sout_ref, tail_ref, xbuf, act, st) = refs
    else:
        (q_ref, k_ref, v_ref, z_ref, gt_ref, cw_ref, al_ref, dt_ref, dn_ref,
         o_ref, sout_ref, tail_ref, xbuf, act, st) = refs
    c = pl.program_id(1)
    nc = pl.num_programs(1)
    rows = nch * CHUNK

    @pl.when(c == 0)
    def _():
        if has_state:
            for p in range(3):
                xbuf[p, 0:8, :] = c0_ref[:, p * WIDTH:(p + 1) * WIDTH]
            st[...] = s0_ref[...]
        else:
            xbuf[:, 0:8, :] = jnp.zeros((3, 8, WIDTH), F32)
            st[...] = jnp.zeros_like(st)

    @pl.when(c > 0)
    def _():
        xbuf[:, 0:8, :] = xbuf[:, rows:rows + 8, :]

    xbuf[0, 8:8 + rows, :] = q_ref[...]
    xbuf[1, 8:8 + rows, :] = k_ref[...]
    xbuf[2, 8:8 + rows, :] = v_ref[...]

    for p in range(3):
        y = None
        for i in range(CONV_W):
            start = 8 - (CONV_W - 1) + i
            term = xbuf[p, start:start + rows, :] * cw_ref[i:i + 1, p * WIDTH:(p + 1) * WIDTH]
            y = term if y is None else y + term
        act[p] = _silu(y)

    ri = lax.broadcasted_iota(jnp.int32, (CHUNK, CHUNK), 0)
    ci = lax.broadcasted_iota(jnp.int32, (CHUNK, CHUNK), 1)
    causal = ri >= ci
    strict = ri > ci
    ltri = causal.astype(F32)
    dn = dn_ref[...]

    chains = []
    for cc in range(nch):
        rs = slice(cc * CHUNK, (cc + 1) * CHUNK)
        gt = gt_ref[rs, :]
        beta_all = jax.nn.sigmoid(gt)
        g_all = -jnp.exp(al_ref[...]) * _softplus(gt + dt_ref[...])
        gcum = jnp.dot(ltri, g_all, precision=lax.Precision.HIGHEST,
                       preferred_element_type=F32)
        gcum_t = jnp.transpose(gcum)
        eg_all = jnp.exp(gcum)
        glast = gcum[CHUNK - 1:CHUNK, :]
        tail_all = jnp.exp(glast - gcum)
        gblk_all = jnp.exp(glast)
        for h in range(HEADS):
            sl = slice(h * HEAD_DIM, (h + 1) * HEAD_DIM)
            qh, kh, vh = act[0, rs, sl], act[1, rs, sl], act[2, rs, sl]
            qh = qh * lax.rsqrt(jnp.sum(qh * qh, axis=-1, keepdims=True) + EPS)
            kh = kh * lax.rsqrt(jnp.sum(kh * kh, axis=-1, keepdims=True) + EPS)
            beta = beta_all[:, h:h + 1]
            gl = HEADS + h
            eg = eg_all[:, gl:gl + 1]
            diff = gcum[:, gl:gl + 1] - gcum_t[gl:gl + 1, :]
            kbeta = kh * beta
            qs = qh * SCALE
            chains.append(dict(
                rs=rs, sl=sl, h=h,
                decay=jnp.exp(jnp.where(causal, diff, -jnp.inf)),
                k16=kh.astype(BF16), kbeta16=kbeta.astype(BF16), qs16=qs.astype(BF16),
                qdec=qs * eg,
                ktail16=(kh * tail_all[:, gl:gl + 1]).astype(BF16),
                gblk=gblk_all[:, gl:gl + 1],
                x=jnp.concatenate([vh * beta, kbeta * eg], axis=1)))

    for ch in chains:
        ch["a"] = jnp.where(strict, _dot_nt(ch["kbeta16"], ch["k16"]) * ch["decay"], 0.0)
    for ch in chains:
        ch["intra16"] = (_dot_nt(ch["qs16"], ch["k16"]) * ch["decay"]).astype(BF16)

    def same_block(size):
        shift = size.bit_length() - 1
        return jnp.right_shift(ri, shift) == jnp.right_shift(ci, shift)

    eye = (ri == ci).astype(F32)
    for ch in chains:
        n1 = jnp.where(same_block(8), -ch["a"], 0.0)
        ch["n1_16"] = n1.astype(BF16)
        ch["p"] = eye + n1
    for ch in chains:
        ch["n2"] = _dot(ch["n1_16"], ch["n1_16"])
    for ch in chains:
        ch["n2_16"] = ch["n2"].astype(BF16)
        ch["n4"] = _dot(ch["n2_16"], ch["n2_16"])
        ch["p"] = ch["p"] + _dot(ch["p"].astype(BF16), ch["n2_16"])
    for ch in chains:
        ch["t"] = ch["p"] + _dot(ch["p"].astype(BF16), ch["n4"].astype(BF16))
    for size in (8, 16, 32):
        couple = jnp.logical_and(same_block(2 * size), jnp.logical_not(same_block(size)))
        for ch in chains:
            ch["t16"] = ch["t"].astype(BF16)
            ch["te"] = _dot(ch["t16"], jnp.where(couple, ch["a"], 0.0).astype(BF16))
        for ch in chains:
            ch["t"] = ch["t"] - _dot(ch["te"].astype(BF16), ch["t16"])
    for ch in chains:
        ch["x"] = _dot(ch["t"].astype(BF16), ch["x"].astype(BF16))

    for ch in chains:
        x16 = ch["x"].astype(BF16)
        ix = _dot(ch["intra16"], x16)
        kx = _dot_tn(ch["ktail16"], x16)
        ch["o_intra"] = ix[:, :HEAD_DIM]
        ch["qeff16"] = (ch["qdec"] - ix[:, HEAD_DIM:]).astype(BF16)
        ch["b"] = kx[:, :HEAD_DIM]
        ch["p16"] = kx[:, HEAD_DIM:].astype(BF16)

    state = [st[h] for h in range(HEADS)]
    for cc in range(nch):
        for ch in chains[cc * HEADS:(cc + 1) * HEADS]:
            h, rs, sl = ch["h"], ch["rs"], ch["sl"]
            s16 = state[h].astype(BF16)
            o = _dot(ch["qeff16"], s16) + ch["o_intra"]
            state[h] = state[h] * ch["gblk"] - _dot(ch["p16"], s16) + ch["b"]
            o = o * lax.rsqrt(jnp.mean(o * o, axis=-1, keepdims=True) + EPS) * dn
            o_ref[rs, sl] = (o * _silu(z_ref[rs, sl])).astype(BF16)
    for h in range(HEADS):
        st[h] = state[h]

    @pl.when(c == nc - 1)
    def _():
        sout_ref[...] = st[...]
        for p in range(3):
            tail_ref[:, p * WIDTH:(p + 1) * WIDTH] = xbuf[p, rows:rows + 8, :]


def _delta(p3, gt3, conv_w, al_row, dt_row, dn_row, conv0, s0):
    b, t, _ = p3.shape
    has_state = s0 is not None
    nch = min(MIXER_CHUNKS_PER_STEP, t // CHUNK)
    rows = nch * CHUNK
    assert t % rows == 0
    col = lambda blk: pl.BlockSpec((None, rows, WIDTH), lambda i, c: (i, c, blk))
    full = lambda shape: pl.BlockSpec(shape, lambda i, c: (0,) * len(shape))
    in_specs = [col(4), col(5), col(6), col(7),
                pl.BlockSpec((None, rows, GATE_LANES), lambda i, c: (i, c, 0)),
                full((CONV_W, 3 * WIDTH)), full((1, GATE_LANES)), full((1, GATE_LANES)),
                full((1, HEAD_DIM))]
    args = [p3, p3, p3, p3, gt3, conv_w, al_row, dt_row, dn_row]
    if has_state:
        in_specs += [pl.BlockSpec((None, 8, 3 * WIDTH), lambda i, c: (i, 0, 0)),
                     pl.BlockSpec((None, HEADS, HEAD_DIM, HEAD_DIM), lambda i, c: (i, 0, 0, 0))]
        args += [conv0, s0]
    return pl.pallas_call(
        functools.partial(_delta_kernel, has_state=has_state, nch=nch),
        grid=(b, t // rows),
        in_specs=in_specs,
        out_specs=[pl.BlockSpec((None, rows, WIDTH), lambda i, c: (i, c, 0)),
                   pl.BlockSpec((None, HEADS, HEAD_DIM, HEAD_DIM), lambda i, c: (i, 0, 0, 0)),
                   pl.BlockSpec((None, 8, 3 * WIDTH), lambda i, c: (i, 0, 0))],
        out_shape=[jax.ShapeDtypeStruct((b, t, WIDTH), BF16),
                   jax.ShapeDtypeStruct((b, HEADS, HEAD_DIM, HEAD_DIM), F32),
                   jax.ShapeDtypeStruct((b, 8, 3 * WIDTH), F32)],
        scratch_shapes=[pltpu.VMEM((3, rows + 8, WIDTH), F32),
                        pltpu.VMEM((3, rows, WIDTH), F32),
                        pltpu.VMEM((HEADS, HEAD_DIM, HEAD_DIM), F32)],
        compiler_params=pltpu.CompilerParams(
            dimension_semantics=("arbitrary", "arbitrary"), vmem_limit_bytes=V7X_VMEM_LIMIT),
        name="delta",
    )(*args)


def _ret_kernel(*refs, has_state, nch):
    if has_state:
        (q_ref, k_ref, v_ref, z_ref, cc_ref, ss_ref, dm_ref, xi_ref, zeta_ref, gb_ref, rn_ref,
         s0_ref, o_ref, sout_ref, st) = refs
    else:
        (q_ref, k_ref, v_ref, z_ref, cc_ref, ss_ref, dm_ref, xi_ref, zeta_ref, gb_ref, rn_ref,
         o_ref, sout_ref, st) = refs
    c = pl.program_id(1)
    nc = pl.num_programs(1)

    @pl.when(c == 0)
    def _():
        if has_state:
            st[...] = s0_ref[...]
        else:
            st[...] = jnp.zeros_like(st)

    chains = []
    for cc in range(nch):
        rs = slice(cc * CHUNK, (cc + 1) * CHUNK)
        cos2 = cc_ref[rs, :]
        sin2 = ss_ref[rs, :]
        for h in range(HEADS):
            sl = slice(h * HEAD_DIM, (h + 1) * HEAD_DIM)
            q = q_ref[rs, sl]
            k = k_ref[rs, sl]
            qh = q * cos2 + pltpu.roll(q, HEAD_DIM // 2, axis=1) * sin2
            kh = (k * cos2 + pltpu.roll(k, HEAD_DIM // 2, axis=1) * sin2) * SCALE
            chains.append(dict(rs=rs, sl=sl, h=h, q16=qh.astype(BF16), k16=kh.astype(BF16),
                               kz16=(kh * zeta_ref[h]).astype(BF16),
                               v16=v_ref[rs, sl].astype(BF16)))
    for ch in chains:
        ch["intra16"] = (_dot_nt(ch["q16"], ch["k16"]) * dm_ref[ch["h"]]).astype(BF16)
    for ch in chains:
        ch["kv"] = _dot_tn(ch["kz16"], ch["v16"])
    for ch in chains:
        ch["o_intra"] = _dot(ch["intra16"], ch["v16"])

    state = [st[h] for h in range(HEADS)]
    for cc in range(nch):
        for ch in chains[cc * HEADS:(cc + 1) * HEADS]:
            h, rs, sl = ch["h"], ch["rs"], ch["sl"]
            ch["o"] = ch["o_intra"] + _dot(ch["q16"], state[h].astype(BF16)) * xi_ref[h]
            state[h] = state[h] * gb_ref[h] + ch["kv"]
    for h in range(HEADS):
        st[h] = state[h]

    for ch in chains:
        ch["d"] = ch["o"] - jnp.mean(ch["o"], axis=-1, keepdims=True)
    for ch in chains:
        ch["var"] = jnp.mean(ch["d"] * ch["d"], axis=-1, keepdims=True)
    for ch in chains:
        rs, sl = ch["rs"], ch["sl"]
        o = ch["d"] * lax.rsqrt(ch["var"] + EPS) * rn_ref[:, sl]
        o_ref[rs, sl] = (o * _silu(z_ref[rs, sl])).astype(BF16)

    @pl.when(c == nc - 1)
    def _():
        sout_ref[...] = st[...]


def _ret(p3, cc, ss, consts, rn_row, s0):
    b, t, _ = p3.shape
    has_state = s0 is not None
    nch = min(MIXER_CHUNKS_PER_STEP, t // CHUNK)
    rows = nch * CHUNK
    assert t % rows == 0
    dmat, xi, zeta, gblk = consts
    col = lambda blk: pl.BlockSpec((None, rows, WIDTH), lambda i, c: (i, c, blk))
    full = lambda shape: pl.BlockSpec(shape, lambda i, c: (0,) * len(shape))
    tab = pl.BlockSpec((rows, HEAD_DIM), lambda i, c: (c, 0))
    in_specs = [col(8), col(9), col(10), col(11), tab, tab,
                full((HEADS, CHUNK, CHUNK)), full((HEADS, CHUNK, HEAD_DIM)),
                full((HEADS, CHUNK, HEAD_DIM)), full((HEADS, 1, HEAD_DIM)), full((1, WIDTH))]
    args = [p3, p3, p3, p3, cc, ss, dmat, xi, zeta, gblk, rn_row]
    if has_state:
        in_specs.append(pl.BlockSpec((None, HEADS, HEAD_DIM, HEAD_DIM), lambda i, c: (i, 0, 0, 0)))
        args.append(s0)
    return pl.pallas_call(
        functools.partial(_ret_kernel, has_state=has_state, nch=nch),
        grid=(b, t // rows),
        in_specs=in_specs,
        out_specs=[pl.BlockSpec((None, rows, WIDTH), lambda i, c: (i, c, 0)),
                   pl.BlockSpec((None, HEADS, HEAD_DIM, HEAD_DIM), lambda i, c: (i, 0, 0, 0))],
        out_shape=[jax.ShapeDtypeStruct((b, t, WIDTH), BF16),
                   jax.ShapeDtypeStruct((b, HEADS, HEAD_DIM, HEAD_DIM), F32)],
        scratch_shapes=[pltpu.VMEM((HEADS, HEAD_DIM, HEAD_DIM), F32)],
        compiler_params=pltpu.CompilerParams(
            dimension_semantics=("arbitrary", "arbitrary"), vmem_limit_bytes=V7X_VMEM_LIMIT),
        name="ret",
    )(*args)


def _merge_kernel(oa_ref, ob_ref, oc_ref, ga_ref, gb_ref, gc_ref, wa_ref, wb_ref, wc_ref,
                  wo_ref, x_ref, g_ref, o_ref):
    merged = (jax.nn.sigmoid(ga_ref[...]) * _dot(oa_ref[...], wa_ref[...])
              + jax.nn.sigmoid(gb_ref[...]) * _dot(ob_ref[...], wb_ref[...])
              + jax.nn.sigmoid(gc_ref[...]) * _dot(oc_ref[...], wc_ref[...]))
    y = _dot(merged.astype(BF16), wo_ref[...])
    y = y * lax.rsqrt(jnp.mean(y * y, axis=-1, keepdims=True) + EPS) * g_ref[...]
    o_ref[...] = x_ref[...] + y


def _merge(oa, ob, oc, p2, x2d, wa, wb, wc, wo, g):
    m = x2d.shape[0]
    tm = min(m, 256)
    row = lambda w: pl.BlockSpec((tm, w), lambda i: (i, 0))
    gate = lambda blk: pl.BlockSpec((tm, D_MODEL), lambda i: (i, blk))
    const = lambda shape: pl.BlockSpec(shape, lambda i: (0, 0), pipeline_mode=pl.Buffered(1))
    return pl.pallas_call(
        _merge_kernel,
        grid=(m // tm,),
        in_specs=[row(WIDTH), row(WIDTH), row(WIDTH), gate(6), gate(7), gate(8),
                  const((WIDTH, D_MODEL)), const((WIDTH, D_MODEL)), const((WIDTH, D_MODEL)),
                  const((D_MODEL, D_MODEL)), row(D_MODEL), const((1, D_MODEL))],
        out_specs=row(D_MODEL),
        out_shape=jax.ShapeDtypeStruct((m, D_MODEL), F32),
        compiler_params=pltpu.CompilerParams(
            dimension_semantics=("arbitrary",), vmem_limit_bytes=V7X_VMEM_LIMIT),
        name="merge",
    )(oa, ob, oc, p2, p2, p2, wa, wb, wc, wo, x2d, g)


def _retention_constants():
    log_gamma = jnp.log1p(-jnp.exp2(-5.0 - jnp.arange(HEADS, dtype=F32)))
    idx = jnp.arange(CHUNK, dtype=F32)
    rel = idx[:, None] - idx[None, :]
    lg = log_gamma[:, None, None]
    dmat = jnp.where(rel >= 0, jnp.exp(lg * jnp.maximum(rel, 0.0)), 0.0)
    xi = jnp.exp(log_gamma[:, None] * (idx + 1.0))
    zeta = jnp.exp(log_gamma[:, None] * (CHUNK - 1.0 - idx))
    gblk = jnp.exp(log_gamma * CHUNK)
    bc = lambda v: jnp.broadcast_to(v[:, :, None], (HEADS, v.shape[1], HEAD_DIM))
    return dmat, bc(xi), bc(zeta), bc(gblk[:, None])


def _rotary_tables(pos):
    half = HEAD_DIM // 2
    inv = ROPE_BASE ** (-jnp.arange(half, dtype=F32) / half)
    ang = pos.astype(F32)[:, None] * inv[None, :]
    cos, sin = jnp.cos(ang), jnp.sin(ang)
    return jnp.concatenate([cos, cos], axis=1), jnp.concatenate([-sin, sin], axis=1)


def _rev_bias(rel_bias):
    n_off = BAND_KEYS + CHUNK - 1
    head = rel_bias[:, MAX_REL - (CHUNK - 1):].astype(F32)
    edge = jnp.broadcast_to(head[:, -1:], (rel_bias.shape[0], n_off - head.shape[1]))
    ext = jnp.concatenate([head, edge], axis=1)
    return jnp.pad(ext[:, ::-1], ((0, 0), (0, BIAS_LANES - n_off)))


def _lane_row(v, lane0):
    return jnp.zeros((1, GATE_LANES), F32).at[0, lane0:lane0 + v.shape[0]].set(v.astype(F32))


def _group_layer(x, pos, cache, conv_buf, s_delta, s_ret, lw, ret_consts):
    (norm_pre, norm_post, w_main, w_gate, bias, conv_w, al_row, dt_row, dn_row, rn_row,
     wa, wb, wc, wo) = lw
    b, t, _ = x.shape
    x2d = x.reshape(b * t, D_MODEL)
    p2, gt2 = _inproj(x2d, norm_pre, w_main, w_gate)
    p3 = p2.reshape(b, t, N_MAIN)
    gt3 = gt2.reshape(b, t, GATE_LANES)

    if cache is not None:
        assert cache[0].shape[1] == BAND_ROWS
        cache = tuple(c.reshape(b, BAND_ROWS, WIDTH) for c in cache)
    oa, new_k, new_v = _attn(p3, bias, cache)
    new_k = new_k.reshape(b, BAND_ROWS, HEADS, HEAD_DIM)
    new_v = new_v.reshape(b, BAND_ROWS, HEADS, HEAD_DIM)

    conv0 = None if conv_buf is None else jnp.pad(conv_buf, ((0, 0), (8 - (CONV_W - 1), 0), (0, 0)))
    ob, new_sd, conv_tail = _delta(p3, gt3, conv_w, al_row, dt_row, dn_row, conv0, s_delta)
    new_conv = conv_tail[:, 8 - (CONV_W - 1):, :]

    cc, ss = _rotary_tables(pos)
    oc, new_sr = _ret(p3, cc, ss, ret_consts, rn_row, s_ret)

    y2d = _merge(oa.reshape(b * t, WIDTH), ob.reshape(b * t, WIDTH), oc.reshape(b * t, WIDTH),
                 p2, x2d, wa, wb, wc, wo, norm_post)
    return y2d.reshape(b, t, D_MODEL), new_k, new_v, new_conv, new_sd, new_sr


def kernel(x_prompt, x_sample, cache_attn_k, cache_attn_v, state_conv, state_delta, state_ret,
           norm_pre, norm_post, w_in, attn_rel_bias, conv_w, delta_a_log, delta_dt_bias,
           delta_norm, ret_norm, w_branch_a, w_branch_b, w_branch_c, w_out):
    depth = w_in.shape[0]
    tp, ts = x_prompt.shape[1], x_sample.shape[1]
    pos_p = jnp.arange(tp, dtype=jnp.int32)
    pos_s = PAST_LEN + jnp.arange(ts, dtype=jnp.int32)
    ret_consts = _retention_constants()
    xp, xs = x_prompt, x_sample
    acc_p = [[] for _ in range(5)]
    acc_s = [[] for _ in range(5)]
    for l in range(depth):
        w = w_in[l]
        w_main = jnp.concatenate([w[:, :GATE_COL0], w[:, GATE_COL0 + 2 * HEADS:]], axis=1).astype(BF16)
        w_gate = jnp.pad(w[:, GATE_COL0:GATE_COL0 + 2 * HEADS],
                         ((0, 0), (0, GATE_LANES - 2 * HEADS))).astype(BF16)
        lw = (norm_pre[l][None, :], norm_post[l][None, :], w_main, w_gate,
              _rev_bias(attn_rel_bias[l]), conv_w[l],
              _lane_row(delta_a_log[l], HEADS), _lane_row(delta_dt_bias[l], HEADS),
              delta_norm[l][None, :].astype(F32), ret_norm[l].reshape(1, WIDTH).astype(F32),
              w_branch_a[l].astype(BF16), w_branch_b[l].astype(BF16), w_branch_c[l].astype(BF16),
              w_out[l].astype(BF16))
        xp, *st_p = _group_layer(xp, pos_p, None, None, None, None, lw, ret_consts)
        xs, *st_s = _group_layer(xs, pos_s, (cache_attn_k[l], cache_attn_v[l]), state_conv[l],
                                 state_delta[l], state_ret[l], lw, ret_consts)
        for acc, val in zip(acc_p, st_p):
            acc.append(val)
        for acc, val in zip(acc_s, st_s):
            acc.append(val)
    outs_p = [jnp.stack(a, axis=0) for a in acc_p]
    outs_s = [jnp.stack(a, axis=0) for a in acc_s]
    return (xp, xs, *outs_p, *outs_s)
```

```python
import functools

import jax
import jax.numpy as jnp
import numpy as np
from jax import lax
from jax.experimental import pallas as pl
from jax.experimental.pallas import tpu as pltpu

F32 = jnp.float32
BF16 = jnp.bfloat16

D_MODEL = 2048
CHUNK = 64
HEAD_DIM = 128
HEADS = 8
WIDTH = HEADS * HEAD_DIM
BAND_CHUNKS = 8
BAND_ROWS = BAND_CHUNKS * CHUNK
BAND_KEYS = BAND_ROWS + CHUNK
BIAS_LANES = 640
MAX_REL = 128
CONV_W = 4
PAST_LEN = 2048
ROPE_BASE = 10000.0
EPS = 1e-6
NEG_INF = -1e30
SCALE = HEAD_DIM ** -0.5

N_MAIN = 18 * WIDTH
GATE_COL0 = 8 * WIDTH
GATE_LANES = 128
V7X_VMEM_LIMIT = 56 * 1024 * 1024
MIXER_CHUNKS_PER_STEP = 4

_NT = (((1,), (1,)), ((), ()))
_TN = (((0,), (0,)), ((), ()))


def _dot(a, b):
    return jnp.dot(a, b, preferred_element_type=F32)


def _dot_nt(a, b):
    return lax.dot_general(a, b, _NT, preferred_element_type=F32)


def _dot_tn(a, b):
    return lax.dot_general(a, b, _TN, preferred_element_type=F32)


def _silu(x):
    return x * jax.nn.sigmoid(x)


def _softplus(x):
    return jnp.maximum(x, 0.0) + jnp.log1p(jnp.exp(-jnp.abs(x)))


PREP_ROWS = 512
N_GATE = 2 * HEADS
assert GATE_COL0 % WIDTH == 0 and N_GATE <= GATE_LANES


def _weight_prep_kernel(a_ref, b_ref, o_ref, og_ref):
    n = pl.program_id(1)
    first_shifted = GATE_COL0 // WIDTH

    @pl.when(n < first_shifted)
    def _():
        o_ref[...] = a_ref[...].astype(BF16)

    @pl.when(n >= first_shifted)
    def _():
        wide = jnp.concatenate([a_ref[...], b_ref[...]], axis=1)
        o_ref[...] = wide[:, N_GATE:N_GATE + WIDTH].astype(BF16)

    @pl.when(n == first_shifted)
    def _():
        lane = lax.broadcasted_iota(jnp.int32, (PREP_ROWS, GATE_LANES), 1)
        og_ref[...] = jnp.where(lane < N_GATE, a_ref[:, 0:GATE_LANES], 0.0).astype(BF16)


def _weight_prep(w_in, layer):
    assert w_in.shape[2] == N_MAIN + N_GATE
    lanes_per_block = WIDTH // GATE_LANES
    return pl.pallas_call(
        _weight_prep_kernel,
        grid=(D_MODEL // PREP_ROWS, N_MAIN // WIDTH),
        in_specs=[pl.BlockSpec((None, PREP_ROWS, WIDTH), lambda r, n: (layer, r, n)),
                  pl.BlockSpec((None, PREP_ROWS, GATE_LANES),
                               lambda r, n: (layer, r, (n + 1) * lanes_per_block))],
        out_specs=[pl.BlockSpec((PREP_ROWS, WIDTH), lambda r, n: (r, n)),
                   pl.BlockSpec((PREP_ROWS, GATE_LANES), lambda r, n: (r, 0))],
        out_shape=[jax.ShapeDtypeStruct((D_MODEL, N_MAIN), BF16),
                   jax.ShapeDtypeStruct((D_MODEL, GATE_LANES), BF16)],
        compiler_params=pltpu.CompilerParams(dimension_semantics=("arbitrary", "arbitrary")),
        name="wprep",
    )(w_in, w_in)


def _inproj_kernel(x_ref, g_ref, w_ref, wg_ref, o_ref, og_ref, h_ref):
    @pl.when(pl.program_id(1) == 0)
    def _():
        x = x_ref[...]
        ms = jnp.mean(x * x, axis=-1, keepdims=True)
        h = (x * lax.rsqrt(ms + EPS) * g_ref[...]).astype(BF16)
        h_ref[...] = h
        og_ref[...] = _dot(h, wg_ref[...])

    o_ref[...] = _dot(h_ref[...], w_ref[...])


def _inproj(x2d, g, w_main, w_gate):
    m = x2d.shape[0]
    tm = min(m, 1024)
    tn = 1024
    return pl.pallas_call(
        _inproj_kernel,
        grid=(m // tm, N_MAIN // tn),
        in_specs=[
            pl.BlockSpec((tm, D_MODEL), lambda i, n: (i, 0)),
            pl.BlockSpec((1, D_MODEL), lambda i, n: (0, 0)),
            pl.BlockSpec((D_MODEL, tn), lambda i, n: (0, n)),
            pl.BlockSpec((D_MODEL, GATE_LANES), lambda i, n: (0, 0)),
        ],
        out_specs=[
            pl.BlockSpec((tm, tn), lambda i, n: (i, n)),
            pl.BlockSpec((tm, GATE_LANES), lambda i, n: (i, 0)),
        ],
        out_shape=[
            jax.ShapeDtypeStruct((m, N_MAIN), F32),
            jax.ShapeDtypeStruct((m, GATE_LANES), F32),
        ],
        scratch_shapes=[pltpu.VMEM((tm, D_MODEL), BF16)],
        compiler_params=pltpu.CompilerParams(
            dimension_semantics=("arbitrary", "arbitrary"),
            vmem_limit_bytes=V7X_VMEM_LIMIT),
        name="inproj",
    )(x2d, g, w_main, w_gate)


def _attn_kernel(*refs, nq, has_cache):
    if has_cache:
        (q_ref, ko_ref, vo_ref, z_ref, rb_ref, kc_ref, vc_ref,
         o_ref, nk_ref, nv_ref, kcat, vcat, bias_scr) = refs
    else:
        (q_ref, ko_ref, vo_ref, z_ref, rb_ref,
         o_ref, nk_ref, nv_ref, kcat, vcat, bias_scr) = refs
    t = pl.program_id(1)
    nt = pl.num_programs(1)
    rows = nq * CHUNK

    @pl.when(jnp.logical_and(pl.program_id(0) == 0, t == 0))
    def _():
        for h in range(HEADS):
            wide = jnp.broadcast_to(rb_ref[h:h + 1, :], (CHUNK, BIAS_LANES))
            bias_scr[h] = pltpu.roll(wide, BIAS_LANES - (CHUNK - 1), 1, stride=1, stride_axis=0)

    if has_cache:
        for h in range(HEADS):
            sl = slice(h * HEAD_DIM, (h + 1) * HEAD_DIM)
            kcat[0:BAND_ROWS, sl] = kc_ref[:, h, :].astype(BF16)
            vcat[0:BAND_ROWS, sl] = vc_ref[:, h, :].astype(BF16)
    else:
        @pl.when(t == 0)
        def _():
            kcat[0:BAND_ROWS, :] = jnp.zeros((BAND_ROWS, WIDTH), BF16)
            vcat[0:BAND_ROWS, :] = jnp.zeros((BAND_ROWS, WIDTH), BF16)

        @pl.when(t > 0)
        def _():
            kcat[0:BAND_ROWS, :] = kcat[rows:rows + BAND_ROWS, :]
            vcat[0:BAND_ROWS, :] = vcat[rows:rows + BAND_ROWS, :]
    kcat[BAND_ROWS:BAND_ROWS + rows, :] = ko_ref[...].astype(BF16)
    vcat[BAND_ROWS:BAND_ROWS + rows, :] = vo_ref[...].astype(BF16)

    col = lax.broadcasted_iota(jnp.int32, (CHUNK, BAND_KEYS), 1)
    for h in range(HEADS):
        sl = slice(h * HEAD_DIM, (h + 1) * HEAD_DIM)
        bias = bias_scr[h][:, :BAND_KEYS]
        scores = []
        for j in range(nq):
            r0 = j * CHUNK
            q = q_ref[r0:r0 + CHUNK, sl].astype(BF16)
            scores.append(_dot_nt(q, kcat[r0:r0 + BAND_KEYS, sl]))
        probs = []
        for j in range(nq):
            s = scores[j] * SCALE + bias
            if not has_cache:
                first_valid = jnp.where(t == 0, BAND_ROWS - j * CHUNK, 0)
                s = jnp.where(col < first_valid, NEG_INF, s)
            m = jnp.max(s, axis=-1, keepdims=True)
            p = jnp.exp(s - m)
            probs.append((p.astype(BF16), jnp.sum(p, axis=-1, keepdims=True)))
        outs = []
        for j in range(nq):
            r0 = j * CHUNK
            outs.append(_dot(probs[j][0], vcat[r0:r0 + BAND_KEYS, sl]))
        for j in range(nq):
            r0 = j * CHUNK
            o = outs[j] / probs[j][1]
            o_ref[r0:r0 + CHUNK, sl] = (o * _silu(z_ref[r0:r0 + CHUNK, sl])).astype(BF16)

    @pl.when(t == nt - 1)
    def _():
        if rows < BAND_ROWS:
            nk_ref[0:BAND_ROWS - rows] = kc_ref[rows:BAND_ROWS]
            nv_ref[0:BAND_ROWS - rows] = vc_ref[rows:BAND_ROWS]
        for h in range(HEADS):
            sl = slice(h * HEAD_DIM, (h + 1) * HEAD_DIM)
            nk_ref[BAND_ROWS - rows:BAND_ROWS, h, :] = ko_ref[:, sl]
            nv_ref[BAND_ROWS - rows:BAND_ROWS, h, :] = vo_ref[:, sl]


def _attn(p3, rev_bias, cache):
    b, t, _ = p3.shape
    has_cache = cache is not None
    rows = CHUNK if has_cache else BAND_ROWS
    assert t % rows == 0 and (not has_cache or t == CHUNK)
    nq = rows // CHUNK
    cur = lambda blk: pl.BlockSpec((None, rows, WIDTH), lambda i, tt: (i, tt, blk))
    newest = pl.BlockSpec((None, BAND_ROWS, HEADS, HEAD_DIM), lambda i, tt: (i, 0, 0, 0))
    in_specs = [cur(0), cur(1), cur(2), cur(3),
                pl.BlockSpec((HEADS, BIAS_LANES), lambda i, tt: (0, 0))]
    args = [p3, p3, p3, p3, rev_bias]
    if has_cache:
        ck, cv, layer = cache
        assert ck.shape[2:] == (BAND_ROWS, HEADS, HEAD_DIM)
        cached = pl.BlockSpec((None, None, BAND_ROWS, HEADS, HEAD_DIM),
                              lambda i, tt: (layer, i, 0, 0, 0))
        in_specs += [cached, cached]
        args += [ck, cv]
    return pl.pallas_call(
        functools.partial(_attn_kernel, nq=nq, has_cache=has_cache),
        grid=(b, t // rows),
        in_specs=in_specs,
        out_specs=[cur(0), newest, newest],
        out_shape=[jax.ShapeDtypeStruct((b, t, WIDTH), BF16),
                   jax.ShapeDtypeStruct((b, BAND_ROWS, HEADS, HEAD_DIM), F32),
                   jax.ShapeDtypeStruct((b, BAND_ROWS, HEADS, HEAD_DIM), F32)],
        scratch_shapes=[pltpu.VMEM((BAND_ROWS + rows, WIDTH), BF16),
                        pltpu.VMEM((BAND_ROWS + rows, WIDTH), BF16),
                        pltpu.VMEM((HEADS, CHUNK, BIAS_LANES), F32)],
        compiler_params=pltpu.CompilerParams(
            dimension_semantics=("arbitrary", "arbitrary"), vmem_limit_bytes=V7X_VMEM_LIMIT),
        name="attn",
    )(*args)


def _delta_kernel(*refs, has_state, nch):
    if has_state:
        (q_ref, k_ref, v_ref, z_ref, gt_ref, cw_ref, al_ref, dt_ref, dn_ref, c0_ref, s0_ref,
         o_ref, sout_ref, tail_ref, xbuf, act, st) = refs
    else:
        (q_ref, k_ref, v_ref, z_ref, gt_ref, cw_ref, al_ref, dt_ref, dn_ref,
         o_ref, sout_ref, tail_ref, xbuf, act, st) = refs
    c = pl.program_id(1)
    nc = pl.num_programs(1)
    rows = nch * CHUNK

    @pl.when(c == 0)
    def _():
        if has_state:
            for p in range(3):
                xbuf[p, 0:8, :] = c0_ref[:, p * WIDTH:(p + 1) * WIDTH]
            st[...] = s0_ref[...]
        else:
            xbuf[:, 0:8, :] = jnp.zeros((3, 8, WIDTH), F32)
            st[...] = jnp.zeros_like(st)

    @pl.when(c > 0)
    def _():
        xbuf[:, 0:8, :] = xbuf[:, rows:rows + 8, :]

    xbuf[0, 8:8 + rows, :] = q_ref[...]
    xbuf[1, 8:8 + rows, :] = k_ref[...]
    xbuf[2, 8:8 + rows, :] = v_ref[...]

    for p in range(3):
        y = None
        for i in range(CONV_W):
            start = 8 - (CONV_W - 1) + i
            term = xbuf[p, start:start + rows, :] * cw_ref[i:i + 1, p * WIDTH:(p + 1) * WIDTH]
            y = term if y is None else y + term
        act[p] = _silu(y)

    ri = lax.broadcasted_iota(jnp.int32, (CHUNK, CHUNK), 0)
    ci = lax.broadcasted_iota(jnp.int32, (CHUNK, CHUNK), 1)
    causal = ri >= ci
    strict = ri > ci
    ltri = causal.astype(F32)
    dn = dn_ref[...]

    chains = []
    for cc in range(nch):
        rs = slice(cc * CHUNK, (cc + 1) * CHUNK)
        gt = gt_ref[rs, :]
        beta_all = jax.nn.sigmoid(gt)
        g_all = -jnp.exp(al_ref[...]) * _softplus(gt + dt_ref[...])
        gcum = jnp.dot(ltri, g_all, precision=lax.Precision.HIGHEST,
                       preferred_element_type=F32)
        gcum_t = jnp.transpose(gcum)
        eg_all = jnp.exp(gcum)
        glast = gcum[CHUNK - 1:CHUNK, :]
        tail_all = jnp.exp(glast - gcum)
        gblk_all = jnp.exp(glast)
        for h in range(HEADS):
            sl = slice(h * HEAD_DIM, (h + 1) * HEAD_DIM)
            qh, kh, vh = act[0, rs, sl], act[1, rs, sl], act[2, rs, sl]
            qh = qh * lax.rsqrt(jnp.sum(qh * qh, axis=-1, keepdims=True) + EPS)
            kh = kh * lax.rsqrt(jnp.sum(kh * kh, axis=-1, keepdims=True) + EPS)
            beta = beta_all[:, h:h + 1]
            gl = HEADS + h
            eg = eg_all[:, gl:gl + 1]
            diff = gcum[:, gl:gl + 1] - gcum_t[gl:gl + 1, :]
            kbeta = kh * beta
            qs = qh * SCALE
            chains.append(dict(
                rs=rs, sl=sl, h=h,
                decay=jnp.exp(jnp.where(causal, diff, -jnp.inf)),
                k16=kh.astype(BF16), kbeta16=kbeta.astype(BF16), qs16=qs.astype(BF16),
                qdec=qs * eg,
                ktail16=(kh * tail_all[:, gl:gl + 1]).astype(BF16),
                gblk=gblk_all[:, gl:gl + 1],
                x=jnp.concatenate([vh * beta, kbeta * eg], axis=1)))

    for ch in chains:
        ch["a"] = jnp.where(strict, _dot_nt(ch["kbeta16"], ch["k16"]) * ch["decay"], 0.0)
    for ch in chains:
        ch["intra16"] = (_dot_nt(ch["qs16"], ch["k16"]) * ch["decay"]).astype(BF16)

    def same_block(size):
        shift = size.bit_length() - 1
        return jnp.right_shift(ri, shift) == jnp.right_shift(ci, shift)

    eye = (ri == ci).astype(F32)
    for ch in chains:
        n1 = jnp.where(same_block(8), -ch["a"], 0.0)
        ch["n1_16"] = n1.astype(BF16)
        ch["p"] = eye + n1
    for ch in chains:
        ch["n2"] = _dot(ch["n1_16"], ch["n1_16"])
    for ch in chains:
        ch["n2_16"] = ch["n2"].astype(BF16)
        ch["n4"] = _dot(ch["n2_16"], ch["n2_16"])
        ch["p"] = ch["p"] + _dot(ch["p"].astype(BF16), ch["n2_16"])
    for ch in chains:
        ch["t"] = ch["p"] + _dot(ch["p"].astype(BF16), ch["n4"].astype(BF16))
    for size in (8, 16, 32):
        couple = jnp.logical_and(same_block(2 * size), jnp.logical_not(same_block(size)))
        for ch in chains:
            ch["t16"] = ch["t"].astype(BF16)
            ch["te"] = _dot(ch["t16"], jnp.where(couple, ch["a"], 0.0).astype(BF16))
        for ch in chains:
            ch["t"] = ch["t"] - _dot(ch["te"].astype(BF16), ch["t16"])
    for ch in chains:
        ch["x"] = _dot(ch["t"].astype(BF16), ch["x"].astype(BF16))

    for ch in chains:
        x16 = ch["x"].astype(BF16)
        ix = _dot(ch["intra16"], x16)
        kx = _dot_tn(ch["ktail16"], x16)
        ch["o_intra"] = ix[:, :HEAD_DIM]
        ch["qeff16"] = (ch["qdec"] - ix[:, HEAD_DIM:]).astype(BF16)
        ch["b"] = kx[:, :HEAD_DIM]
        ch["p16"] = kx[:, HEAD_DIM:].astype(BF16)

    state = [st[h] for h in range(HEADS)]
    for cc in range(nch):
        for ch in chains[cc * HEADS:(cc + 1) * HEADS]:
            h, rs, sl = ch["h"], ch["rs"], ch["sl"]
            s16 = state[h].astype(BF16)
            o = _dot(ch["qeff16"], s16) + ch["o_intra"]
            state[h] = state[h] * ch["gblk"] - _dot(ch["p16"], s16) + ch["b"]
            o = o * lax.rsqrt(jnp.mean(o * o, axis=-1, keepdims=True) + EPS) * dn
            o_ref[rs, sl] = (o * _silu(z_ref[rs, sl])).astype(BF16)
    for h in range(HEADS):
        st[h] = state[h]

    @pl.when(c == nc - 1)
    def _():
        sout_ref[...] = st[...]
        for p in range(3):
            tail_ref[:, p * WIDTH:(p + 1) * WIDTH] = xbuf[p, rows:rows + 8, :]


def _delta(p3, gt3, conv_w, al_row, dt_row, dn_row, conv0, s0):
    b, t, _ = p3.shape
    has_state = s0 is not None
    nch = min(MIXER_CHUNKS_PER_STEP, t // CHUNK)
    rows = nch * CHUNK
    assert t % rows == 0
    col = lambda blk: pl.BlockSpec((None, rows, WIDTH), lambda i, c: (i, c, blk))
    full = lambda shape: pl.BlockSpec(shape, lambda i, c: (0,) * len(shape))
    in_specs = [col(4), col(5), col(6), col(7),
                pl.BlockSpec((None, rows, GATE_LANES), lambda i, c: (i, c, 0)),
                full((CONV_W, 3 * WIDTH)), full((1, GATE_LANES)), full((1, GATE_LANES)),
                full((1, HEAD_DIM))]
    args = [p3, p3, p3, p3, gt3, conv_w, al_row, dt_row, dn_row]
    if has_state:
        in_specs += [pl.BlockSpec((None, 8, 3 * WIDTH), lambda i, c: (i, 0, 0)),
                     pl.BlockSpec((None, HEADS, HEAD_DIM, HEAD_DIM), lambda i, c: (i, 0, 0, 0))]
        args += [conv0, s0]
    return pl.pallas_call(
        functools.partial(_delta_kernel, has_state=has_state, nch=nch),
        grid=(b, t // rows),
        in_specs=in_specs,
        out_specs=[pl.BlockSpec((None, rows, WIDTH), lambda i, c: (i, c, 0)),
                   pl.BlockSpec((None, HEADS, HEAD_DIM, HEAD_DIM), lambda i, c: (i, 0, 0, 0)),
                   pl.BlockSpec((None, 8, 3 * WIDTH), lambda i, c: (i, 0, 0))],
        out_shape=[jax.ShapeDtypeStruct((b, t, WIDTH), BF16),
                   jax.ShapeDtypeStruct((b, HEADS, HEAD_DIM, HEAD_DIM), F32),
                   jax.ShapeDtypeStruct((b, 8, 3 * WIDTH), F32)],
        scratch_shapes=[pltpu.VMEM((3, rows + 8, WIDTH), F32),
                        pltpu.VMEM((3, rows, WIDTH), F32),
                        pltpu.VMEM((HEADS, HEAD_DIM, HEAD_DIM), F32)],
        compiler_params=pltpu.CompilerParams(
            dimension_semantics=("arbitrary", "arbitrary"), vmem_limit_bytes=V7X_VMEM_LIMIT),
        name="delta",
    )(*args)


def _ret_kernel(*refs, has_state, nch):
    if has_state:
        (q_ref, k_ref, v_ref, z_ref, cc_ref, ss_ref, dm_ref, xi_ref, zeta_ref, gb_ref, rn_ref,
         s0_ref, o_ref, sout_ref, st) = refs
    else:
        (q_ref, k_ref, v_ref, z_ref, cc_ref, ss_ref, dm_ref, xi_ref, zeta_ref, gb_ref, rn_ref,
         o_ref, sout_ref, st) = refs
    c = pl.program_id(1)
    nc = pl.num_programs(1)

    @pl.when(c == 0)
    def _():
        if has_state:
            st[...] = s0_ref[...]
        else:
            st[...] = jnp.zeros_like(st)

    chains = []
    for cc in range(nch):
        rs = slice(cc * CHUNK, (cc + 1) * CHUNK)
        cos2 = cc_ref[rs, :]
        sin2 = ss_ref[rs, :]
        for h in range(HEADS):
            sl = slice(h * HEAD_DIM, (h + 1) * HEAD_DIM)
            q = q_ref[rs, sl]
            k = k_ref[rs, sl]
            qh = q * cos2 + pltpu.roll(q, HEAD_DIM // 2, axis=1) * sin2
            kh = (k * cos2 + pltpu.roll(k, HEAD_DIM // 2, axis=1) * sin2) * SCALE
            chains.append(dict(rs=rs, sl=sl, h=h, q16=qh.astype(BF16), k16=kh.astype(BF16),
                               kz16=(kh * zeta_ref[h]).astype(BF16),
                               v16=v_ref[rs, sl].astype(BF16)))
    for ch in chains:
        ch["intra16"] = (_dot_nt(ch["q16"], ch["k16"]) * dm_ref[ch["h"]]).astype(BF16)
    for ch in chains:
        ch["kv"] = _dot_tn(ch["kz16"], ch["v16"])
    for ch in chains:
        ch["o_intra"] = _dot(ch["intra16"], ch["v16"])

    state = [st[h] for h in range(HEADS)]
    for cc in range(nch):
        for ch in chains[cc * HEADS:(cc + 1) * HEADS]:
            h, rs, sl = ch["h"], ch["rs"], ch["sl"]
            ch["o"] = ch["o_intra"] + _dot(ch["q16"], state[h].astype(BF16)) * xi_ref[h]
            state[h] = state[h] * gb_ref[h] + ch["kv"]
    for h in range(HEADS):
        st[h] = state[h]

    for ch in chains:
        ch["d"] = ch["o"] - jnp.mean(ch["o"], axis=-1, keepdims=True)
    for ch in chains:
        ch["var"] = jnp.mean(ch["d"] * ch["d"], axis=-1, keepdims=True)
    for ch in chains:
        rs, sl = ch["rs"], ch["sl"]
        o = ch["d"] * lax.rsqrt(ch["var"] + EPS) * rn_ref[:, sl]
        o_ref[rs, sl] = (o * _silu(z_ref[rs, sl])).astype(BF16)

    @pl.when(c == nc - 1)
    def _():
        sout_ref[...] = st[...]


def _ret(p3, cc, ss, consts, rn_row, s0):
    b, t, _ = p3.shape
    has_state = s0 is not None
    nch = min(MIXER_CHUNKS_PER_STEP, t // CHUNK)
    rows = nch * CHUNK
    assert t % rows == 0
    dmat, xi, zeta, gblk = consts
    col = lambda blk: pl.BlockSpec((None, rows, WIDTH), lambda i, c: (i, c, blk))
    full = lambda shape: pl.BlockSpec(shape, lambda i, c: (0,) * len(shape))
    tab = pl.BlockSpec((rows, HEAD_DIM), lambda i, c: (c, 0))
    in_specs = [col(8), col(9), col(10), col(11), tab, tab,
                full((HEADS, CHUNK, CHUNK)), full((HEADS, CHUNK, HEAD_DIM)),
                full((HEADS, CHUNK, HEAD_DIM)), full((HEADS, 1, HEAD_DIM)), full((1, WIDTH))]
    args = [p3, p3, p3, p3, cc, ss, dmat, xi, zeta, gblk, rn_row]
    if has_state:
        in_specs.append(pl.BlockSpec((None, HEADS, HEAD_DIM, HEAD_DIM), lambda i, c: (i, 0, 0, 0)))
        args.append(s0)
    return pl.pallas_call(
        functools.partial(_ret_kernel, has_state=has_state, nch=nch),
        grid=(b, t // rows),
        in_specs=in_specs,
        out_specs=[pl.BlockSpec((None, rows, WIDTH), lambda i, c: (i, c, 0)),
                   pl.BlockSpec((None, HEADS, HEAD_DIM, HEAD_DIM), lambda i, c: (i, 0, 0, 0))],
        out_shape=[jax.ShapeDtypeStruct((b, t, WIDTH), BF16),
                   jax.ShapeDtypeStruct((b, HEADS, HEAD_DIM, HEAD_DIM), F32)],
        scratch_shapes=[pltpu.VMEM((HEADS, HEAD_DIM, HEAD_DIM), F32)],
        compiler_params=pltpu.CompilerParams(
            dimension_semantics=("arbitrary", "arbitrary"), vmem_limit_bytes=V7X_VMEM_LIMIT),
        name="ret",
    )(*args)


def _merge_kernel(oa_ref, ob_ref, oc_ref, ga_ref, gb_ref, gc_ref, wa_ref, wb_ref, wc_ref,
                  wo_ref, x_ref, g_ref, o_ref):
    merged = (jax.nn.sigmoid(ga_ref[...]) * _dot(oa_ref[...], wa_ref[...])
              + jax.nn.sigmoid(gb_ref[...]) * _dot(ob_ref[...], wb_ref[...])
              + jax.nn.sigmoid(gc_ref[...]) * _dot(oc_ref[...], wc_ref[...]))
    y = _dot(merged.astype(BF16), wo_ref[...])
    y = y * lax.rsqrt(jnp.mean(y * y, axis=-1, keepdims=True) + EPS) * g_ref[...]
    o_ref[...] = x_ref[...] + y


def _merge(oa, ob, oc, p2, x2d, wa, wb, wc, wo, g):
    m = x2d.shape[0]
    tm = min(m, 256)
    row = lambda w: pl.BlockSpec((tm, w), lambda i: (i, 0))
    gate = lambda blk: pl.BlockSpec((tm, D_MODEL), lambda i: (i, blk))
    const = lambda shape: pl.BlockSpec(shape, lambda i: (0, 0), pipeline_mode=pl.Buffered(1))
    return pl.pallas_call(
        _merge_kernel,
        grid=(m // tm,),
        in_specs=[row(WIDTH), row(WIDTH), row(WIDTH), gate(6), gate(7), gate(8),
                  const((WIDTH, D_MODEL)), const((WIDTH, D_MODEL)), const((WIDTH, D_MODEL)),
                  const((D_MODEL, D_MODEL)), row(D_MODEL), const((1, D_MODEL))],
        out_specs=row(D_MODEL),
        out_shape=jax.ShapeDtypeStruct((m, D_MODEL), F32),
        compiler_params=pltpu.CompilerParams(
            dimension_semantics=("arbitrary",), vmem_limit_bytes=V7X_VMEM_LIMIT),
        name="merge",
    )(oa, ob, oc, p2, p2, p2, wa, wb, wc, wo, x2d, g)


def _retention_constants():
    log_gamma = jnp.log1p(-jnp.exp2(-5.0 - jnp.arange(HEADS, dtype=F32)))
    idx = jnp.arange(CHUNK, dtype=F32)
    rel = idx[:, None] - idx[None, :]
    lg = log_gamma[:, None, None]
    dmat = jnp.where(rel >= 0, jnp.exp(lg * jnp.maximum(rel, 0.0)), 0.0)
    xi = jnp.exp(log_gamma[:, None] * (idx + 1.0))
    zeta = jnp.exp(log_gamma[:, None] * (CHUNK - 1.0 - idx))
    gblk = jnp.exp(log_gamma * CHUNK)
    bc = lambda v: jnp.broadcast_to(v[:, :, None], (HEADS, v.shape[1], HEAD_DIM))
    return dmat, bc(xi), bc(zeta), bc(gblk[:, None])


def _rotary_tables(pos):
    half = HEAD_DIM // 2
    inv = ROPE_BASE ** (-jnp.arange(half, dtype=F32) / half)
    ang = pos.astype(F32)[:, None] * inv[None, :]
    cos, sin = jnp.cos(ang), jnp.sin(ang)
    return jnp.concatenate([cos, cos], axis=1), jnp.concatenate([-sin, sin], axis=1)


def _rev_bias(rel_bias):
    n_off = BAND_KEYS + CHUNK - 1
    head = rel_bias[:, MAX_REL - (CHUNK - 1):].astype(F32)
    edge = jnp.broadcast_to(head[:, -1:], (rel_bias.shape[0], n_off - head.shape[1]))
    ext = jnp.concatenate([head, edge], axis=1)
    return jnp.pad(ext[:, ::-1], ((0, 0), (0, BIAS_LANES - n_off)))


def _lane_row(v, lane0):
    return jnp.zeros((1, GATE_LANES), F32).at[0, lane0:lane0 + v.shape[0]].set(v.astype(F32))


def _group_layer(x, pos, cache, conv_buf, s_delta, s_ret, lw, ret_consts):
    (norm_pre, norm_post, w_main, w_gate, bias, conv_w, al_row, dt_row, dn_row, rn_row,
     wa, wb, wc, wo) = lw
    b, t, _ = x.shape
    x2d = x.reshape(b * t, D_MODEL)
    p2, gt2 = _inproj(x2d, norm_pre, w_main, w_gate)
    p3 = p2.reshape(b, t, N_MAIN)
    gt3 = gt2.reshape(b, t, GATE_LANES)

    oa, new_k, new_v = _attn(p3, bias, cache)

    conv0 = None if conv_buf is None else jnp.pad(conv_buf, ((0, 0), (8 - (CONV_W - 1), 0), (0, 0)))
    ob, new_sd, conv_tail = _delta(p3, gt3, conv_w, al_row, dt_row, dn_row, conv0, s_delta)
    new_conv = conv_tail[:, 8 - (CONV_W - 1):, :]

    cc, ss = _rotary_tables(pos)
    oc, new_sr = _ret(p3, cc, ss, ret_consts, rn_row, s_ret)

    y2d = _merge(oa.reshape(b * t, WIDTH), ob.reshape(b * t, WIDTH), oc.reshape(b * t, WIDTH),
                 p2, x2d, wa, wb, wc, wo, norm_post)
    return y2d.reshape(b, t, D_MODEL), new_k, new_v, new_conv, new_sd, new_sr


def kernel(x_prompt, x_sample, cache_attn_k, cache_attn_v, state_conv, state_delta, state_ret,
           norm_pre, norm_post, w_in, attn_rel_bias, conv_w, delta_a_log, delta_dt_bias,
           delta_norm, ret_norm, w_branch_a, w_branch_b, w_branch_c, w_out):
    depth = w_in.shape[0]
    tp, ts = x_prompt.shape[1], x_sample.shape[1]
    pos_p = jnp.arange(tp, dtype=jnp.int32)
    pos_s = PAST_LEN + jnp.arange(ts, dtype=jnp.int32)
    ret_consts = _retention_constants()
    xp, xs = x_prompt, x_sample
    acc_p = [[] for _ in range(5)]
    acc_s = [[] for _ in range(5)]
    for l in range(depth):
        w_main, w_gate = _weight_prep(w_in, l)
        lw =(norm_pre[l][None, :], norm_post[l][None, :], w_main, w_gate,
              _rev_bias(attn_rel_bias[l]), conv_w[l],
              _lane_row(delta_a_log[l], HEADS), _lane_row(delta_dt_bias[l], HEADS),
              delta_norm[l][None, :].astype(F32), ret_norm[l].reshape(1, WIDTH).astype(F32),
              w_branch_a[l].astype(BF16), w_branch_b[l].astype(BF16), w_branch_c[l].astype(BF16),
              w_out[l].astype(BF16))
        xp, *st_p = _group_layer(xp, pos_p, None, None, None, None, lw, ret_consts)
        xs, *st_s = _group_layer(xs, pos_s, (cache_attn_k, cache_attn_v, l), state_conv[l],
                                 state_delta[l], state_ret[l], lw, ret_consts)
        for acc, val in zip(acc_p, st_p):
            acc.append(val)
        for acc, val in zip(acc_s, st_s):
            acc.append(val)
    outs_p = [jnp.stack(a, axis=0) for a in acc_p]
    outs_s = [jnp.stack(a, axis=0) for a in acc_s]
    return (xp, xs, *outs_p, *outs_s)
```

```python
import functools

import jax
import jax.numpy as jnp
import numpy as np
from jax import lax
from jax.experimental import pallas as pl
from jax.experimental.pallas import tpu as pltpu

F32 = jnp.float32
BF16 = jnp.bfloat16

D_MODEL = 2048
CHUNK = 64
HEAD_DIM = 128
HEADS = 8
WIDTH = HEADS * HEAD_DIM
BAND_CHUNKS = 8
BAND_ROWS = BAND_CHUNKS * CHUNK
BAND_KEYS = BAND_ROWS + CHUNK
BIAS_LANES = 640
MAX_REL = 128
CONV_W = 4
PAST_LEN = 2048
ROPE_BASE = 10000.0
EPS = 1e-6
NEG_INF = -1e30
SCALE = HEAD_DIM ** -0.5

N_MAIN = 18 * WIDTH
GATE_COL0 = 8 * WIDTH
GATE_LANES = 128
V7X_VMEM_LIMIT = 56 * 1024 * 1024
MIXER_CHUNKS_PER_STEP = 4

_NT = (((1,), (1,)), ((), ()))
_TN = (((0,), (0,)), ((), ()))


def _dot(a, b):
    return jnp.dot(a, b, preferred_element_type=F32)


def _dot_nt(a, b):
    return lax.dot_general(a, b, _NT, preferred_element_type=F32)


def _dot_tn(a, b):
    return lax.dot_general(a, b, _TN, preferred_element_type=F32)


def _silu(x):
    return x * jax.nn.sigmoid(x)


def _softplus(x):
    return jnp.maximum(x, 0.0) + jnp.log1p(jnp.exp(-jnp.abs(x)))


PREP_COLS = 512
N_GATE = 2 * HEADS
assert GATE_COL0 % PREP_COLS == 0 and N_GATE % 8 == 0 and N_GATE <= GATE_LANES


def _weight_prep_kernel(a_ref, b_ref, o_ref, og_ref):
    n = pl.program_id(0)
    first_shifted = GATE_COL0 // PREP_COLS

    @pl.when(n < first_shifted)
    def _():
        o_ref[...] = jnp.transpose(a_ref[...]).astype(BF16)

    @pl.when(n >= first_shifted)
    def _():
        src = jnp.concatenate([a_ref[N_GATE:PREP_COLS, :], b_ref[...]], axis=0)
        o_ref[...] = jnp.transpose(src).astype(BF16)

    @pl.when(n == first_shifted)
    def _():
        pad = jnp.zeros((GATE_LANES - N_GATE, D_MODEL), F32)
        og_ref[...] = jnp.concatenate([a_ref[0:N_GATE, :], pad], axis=0).astype(BF16)


def _weight_prep(w_in_t, layer):
    assert w_in_t.shape[1] == N_MAIN + N_GATE
    return pl.pallas_call(
        _weight_prep_kernel,
        grid=(N_MAIN // PREP_COLS,),
        in_specs=[pl.BlockSpec((None, PREP_COLS, D_MODEL), lambda n: (layer, n, 0)),
                  pl.BlockSpec((None, N_GATE, D_MODEL),
                               lambda n: (layer, (n + 1) * (PREP_COLS // N_GATE), 0))],
        out_specs=[pl.BlockSpec((D_MODEL, PREP_COLS), lambda n: (0, n)),
                   pl.BlockSpec((GATE_LANES, D_MODEL), lambda n: (0, 0))],
        out_shape=[jax.ShapeDtypeStruct((D_MODEL, N_MAIN), BF16),
                   jax.ShapeDtypeStruct((GATE_LANES, D_MODEL), BF16)],
        compiler_params=pltpu.CompilerParams(
            dimension_semantics=("arbitrary",), vmem_limit_bytes=V7X_VMEM_LIMIT),
        name="wprep",
    )(w_in_t, w_in_t)


def _inproj_kernel(x_ref, g_ref, w_ref, wg_ref, o_ref, og_ref, h_ref):
    @pl.when(pl.program_id(1) == 0)
    def _():
        x = x_ref[...]
        ms = jnp.mean(x * x, axis=-1, keepdims=True)
        h = (x * lax.rsqrt(ms + EPS) * g_ref[...]).astype(BF16)
        h_ref[...] = h
        og_ref[...] = _dot_nt(h, wg_ref[...])

    o_ref[...] = _dot(h_ref[...], w_ref[...])


def _inproj(x2d, g, w_main, w_gate):
    m = x2d.shape[0]
    tm = min(m, 1024)
    tn = 1024
    return pl.pallas_call(
        _inproj_kernel,
        grid=(m // tm, N_MAIN // tn),
        in_specs=[
            pl.BlockSpec((tm, D_MODEL), lambda i, n: (i, 0)),
            pl.BlockSpec((1, D_MODEL), lambda i, n: (0, 0)),
            pl.BlockSpec((D_MODEL, tn), lambda i, n: (0, n)),
            pl.BlockSpec((GATE_LANES, D_MODEL), lambda i, n: (0, 0)),
        ],
        out_specs=[
            pl.BlockSpec((tm, tn), lambda i, n: (i, n)),
            pl.BlockSpec((tm, GATE_LANES), lambda i, n: (i, 0)),
        ],
        out_shape=[
            jax.ShapeDtypeStruct((m, N_MAIN), F32),
            jax.ShapeDtypeStruct((m, GATE_LANES), F32),
        ],
        scratch_shapes=[pltpu.VMEM((tm, D_MODEL), BF16)],
        compiler_params=pltpu.CompilerParams(
            dimension_semantics=("arbitrary", "arbitrary"),
            vmem_limit_bytes=V7X_VMEM_LIMIT),
        name="inproj",
    )(x2d, g, w_main, w_gate)


def _attn_kernel(*refs, nq, has_cache):
    if has_cache:
        (q_ref, ko_ref, vo_ref, z_ref, rb_ref, kc_ref, vc_ref,
         o_ref, nk_ref, nv_ref, kcat, vcat, bias_scr) = refs
    else:
        (q_ref, ko_ref, vo_ref, z_ref, rb_ref,
         o_ref, nk_ref, nv_ref, kcat, vcat, bias_scr) = refs
    t = pl.program_id(1)
    nt = pl.num_programs(1)
    rows = nq * CHUNK

    @pl.when(jnp.logical_and(pl.program_id(0) == 0, t == 0))
    def _():
        for h in range(HEADS):
            wide = jnp.broadcast_to(rb_ref[h:h + 1, :], (CHUNK, BIAS_LANES))
            bias_scr[h] = pltpu.roll(wide, BIAS_LANES - (CHUNK - 1), 1, stride=1, stride_axis=0)

    if has_cache:
        for h in range(HEADS):
            sl = slice(h * HEAD_DIM, (h + 1) * HEAD_DIM)
            kcat[0:BAND_ROWS, sl] = kc_ref[:, h, :].astype(BF16)
            vcat[0:BAND_ROWS, sl] = vc_ref[:, h, :].astype(BF16)
    else:
        @pl.when(t == 0)
        def _():
            kcat[0:BAND_ROWS, :] = jnp.zeros((BAND_ROWS, WIDTH), BF16)
            vcat[0:BAND_ROWS, :] = jnp.zeros((BAND_ROWS, WIDTH), BF16)

        @pl.when(t > 0)
        def _():
            kcat[0:BAND_ROWS, :] = kcat[rows:rows + BAND_ROWS, :]
            vcat[0:BAND_ROWS, :] = vcat[rows:rows + BAND_ROWS, :]
    kcat[BAND_ROWS:BAND_ROWS + rows, :] = ko_ref[...].astype(BF16)
    vcat[BAND_ROWS:BAND_ROWS + rows, :] = vo_ref[...].astype(BF16)

    col = lax.broadcasted_iota(jnp.int32, (CHUNK, BAND_KEYS), 1)
    for h in range(HEADS):
        sl = slice(h * HEAD_DIM, (h + 1) * HEAD_DIM)
        bias = bias_scr[h][:, :BAND_KEYS]
        scores = []
        for j in range(nq):
            r0 = j * CHUNK
            q = q_ref[r0:r0 + CHUNK, sl].astype(BF16)
            scores.append(_dot_nt(q, kcat[r0:r0 + BAND_KEYS, sl]))
        probs = []
        for j in range(nq):
            s = scores[j] * SCALE + bias
            if not has_cache:
                first_valid = jnp.where(t == 0, BAND_ROWS - j * CHUNK, 0)
                s = jnp.where(col < first_valid, NEG_INF, s)
            m = jnp.max(s, axis=-1, keepdims=True)
            p = jnp.exp(s - m)
            probs.append((p.astype(BF16), jnp.sum(p, axis=-1, keepdims=True)))
        outs = []
        for j in range(nq):
            r0 = j * CHUNK
            outs.append(_dot(probs[j][0], vcat[r0:r0 + BAND_KEYS, sl]))
        for j in range(nq):
            r0 = j * CHUNK
            o = outs[j] / probs[j][1]
            o_ref[r0:r0 + CHUNK, sl] = (o * _silu(z_ref[r0:r0 + CHUNK, sl])).astype(BF16)

    @pl.when(t == nt - 1)
    def _():
        if rows < BAND_ROWS:
            nk_ref[0:BAND_ROWS - rows] = kc_ref[rows:BAND_ROWS]
            nv_ref[0:BAND_ROWS - rows] = vc_ref[rows:BAND_ROWS]
        for h in range(HEADS):
            sl = slice(h * HEAD_DIM, (h + 1) * HEAD_DIM)
            nk_ref[BAND_ROWS - rows:BAND_ROWS, h, :] = ko_ref[:, sl]
            nv_ref[BAND_ROWS - rows:BAND_ROWS, h, :] = vo_ref[:, sl]


def _attn(p3, rev_bias, cache):
    b, t, _ = p3.shape
    has_cache = cache is not None
    rows = CHUNK if has_cache else BAND_ROWS
    assert t % rows == 0 and (not has_cache or t == CHUNK)
    nq = rows // CHUNK
    cur = lambda blk: pl.BlockSpec((None, rows, WIDTH), lambda i, tt: (i, tt, blk))
    newest = pl.BlockSpec((None, BAND_ROWS, HEADS, HEAD_DIM), lambda i, tt: (i, 0, 0, 0))
    in_specs = [cur(0), cur(1), cur(2), cur(3),
                pl.BlockSpec((HEADS, BIAS_LANES), lambda i, tt: (0, 0))]
    args = [p3, p3, p3, p3, rev_bias]
    if has_cache:
        ck, cv, layer = cache
        assert ck.shape[2:] == (BAND_ROWS, HEADS, HEAD_DIM)
        cached = pl.BlockSpec((None, None, BAND_ROWS, HEADS, HEAD_DIM),
                              lambda i, tt: (layer, i, 0, 0, 0))
        in_specs += [cached, cached]
        args += [ck, cv]
    return pl.pallas_call(
        functools.partial(_attn_kernel, nq=nq, has_cache=has_cache),
        grid=(b, t // rows),
        in_specs=in_specs,
        out_specs=[cur(0), newest, newest],
        out_shape=[jax.ShapeDtypeStruct((b, t, WIDTH), BF16),
                   jax.ShapeDtypeStruct((b, BAND_ROWS, HEADS, HEAD_DIM), F32),
                   jax.ShapeDtypeStruct((b, BAND_ROWS, HEADS, HEAD_DIM), F32)],
        scratch_shapes=[pltpu.VMEM((BAND_ROWS + rows, WIDTH), BF16),
                        pltpu.VMEM((BAND_ROWS + rows, WIDTH), BF16),
                        pltpu.VMEM((HEADS, CHUNK, BIAS_LANES), F32)],
        compiler_params=pltpu.CompilerParams(
            dimension_semantics=("arbitrary", "arbitrary"), vmem_limit_bytes=V7X_VMEM_LIMIT),
        name="attn",
    )(*args)


def _delta_kernel(*refs, has_state, nch):
    if has_state:
        (q_ref, k_ref, v_ref, z_ref, gt_ref, cw_ref, al_ref, dt_ref, dn_ref, c0_ref, s0_ref,
         o_ref, sout_ref, tail_ref, xbuf, act, st) = refs
    else:
        (q_ref, k_ref, v_ref, z_ref, gt_ref, cw_ref, al_ref, dt_ref, dn_ref,
         o_ref, sout_ref, tail_ref, xbuf, act, st) = refs
    c = pl.program_id(1)
    nc = pl.num_programs(1)
    rows = nch * CHUNK

    @pl.when(c == 0)
    def _():
        if has_state:
            for p in range(3):
                xbuf[p, 0:8, :] = c0_ref[:, p * WIDTH:(p + 1) * WIDTH]
            st[...] = s0_ref[...]
        else:
            xbuf[:, 0:8, :] = jnp.zeros((3, 8, WIDTH), F32)
            st[...] = jnp.zeros_like(st)

    @pl.when(c > 0)
    def _():
        xbuf[:, 0:8, :] = xbuf[:, rows:rows + 8, :]

    xbuf[0, 8:8 + rows, :] = q_ref[...]
    xbuf[1, 8:8 + rows, :] = k_ref[...]
    xbuf[2, 8:8 + rows, :] = v_ref[...]

    for p in range(3):
        y = None
        for i in range(CONV_W):
            start = 8 - (CONV_W - 1) + i
            term = xbuf[p, start:start + rows, :] * cw_ref[i:i + 1, p * WIDTH:(p + 1) * WIDTH]
            y = term if y is None else y + term
        act[p] = _silu(y)

    ri = lax.broadcasted_iota(jnp.int32, (CHUNK, CHUNK), 0)
    ci = lax.broadcasted_iota(jnp.int32, (CHUNK, CHUNK), 1)
    causal = ri >= ci
    strict = ri > ci
    ltri = causal.astype(F32)
    dn = dn_ref[...]

    chains = []
    for cc in range(nch):
        rs = slice(cc * CHUNK, (cc + 1) * CHUNK)
        gt = gt_ref[rs, :]
        beta_all = jax.nn.sigmoid(gt)
        g_all = -jnp.exp(al_ref[...]) * _softplus(gt + dt_ref[...])
        gcum = jnp.dot(ltri, g_all, precision=lax.Precision.HIGHEST,
                       preferred_element_type=F32)
        gcum_t = jnp.transpose(gcum)
        eg_all = jnp.exp(gcum)
        glast = gcum[CHUNK - 1:CHUNK, :]
        tail_all = jnp.exp(glast - gcum)
        gblk_all = jnp.exp(glast)
        for h in range(HEADS):
            sl = slice(h * HEAD_DIM, (h + 1) * HEAD_DIM)
            qh, kh, vh = act[0, rs, sl], act[1, rs, sl], act[2, rs, sl]
            qh = qh * lax.rsqrt(jnp.sum(qh * qh, axis=-1, keepdims=True) + EPS)
            kh = kh * lax.rsqrt(jnp.sum(kh * kh, axis=-1, keepdims=True) + EPS)
            beta = beta_all[:, h:h + 1]
            gl = HEADS + h
            eg = eg_all[:, gl:gl + 1]
            diff = gcum[:, gl:gl + 1] - gcum_t[gl:gl + 1, :]
            kbeta = kh * beta
            qs = qh * SCALE
            chains.append(dict(
                rs=rs, sl=sl, h=h,
                decay=jnp.exp(jnp.where(causal, diff, -jnp.inf)),
                k16=kh.astype(BF16), kbeta16=kbeta.astype(BF16), qs16=qs.astype(BF16),
                qdec=qs * eg,
                ktail16=(kh * tail_all[:, gl:gl + 1]).astype(BF16),
                gblk=gblk_all[:, gl:gl + 1],
                x=jnp.concatenate([vh * beta, kbeta * eg], axis=1)))

    for ch in chains:
        ch["a"] = jnp.where(strict, _dot_nt(ch["kbeta16"], ch["k16"]) * ch["decay"], 0.0)
    for ch in chains:
        ch["intra16"] = (_dot_nt(ch["qs16"], ch["k16"]) * ch["decay"]).astype(BF16)

    def same_block(size):
        shift = size.bit_length() - 1
        return jnp.right_shift(ri, shift) == jnp.right_shift(ci, shift)

    eye = (ri == ci).astype(F32)
    for ch in chains:
        n1 = jnp.where(same_block(8), -ch["a"], 0.0)
        ch["n1_16"] = n1.astype(BF16)
        ch["p"] = eye + n1
    for ch in chains:
        ch["n2"] = _dot(ch["n1_16"], ch["n1_16"])
    for ch in chains:
        ch["n2_16"] = ch["n2"].astype(BF16)
        ch["n4"] = _dot(ch["n2_16"], ch["n2_16"])
        ch["p"] = ch["p"] + _dot(ch["p"].astype(BF16), ch["n2_16"])
    for ch in chains:
        ch["t"] = ch["p"] + _dot(ch["p"].astype(BF16), ch["n4"].astype(BF16))
    for size in (8, 16, 32):
        couple = jnp.logical_and(same_block(2 * size), jnp.logical_not(same_block(size)))
        for ch in chains:
            ch["t16"] = ch["t"].astype(BF16)
            ch["te"] = _dot(ch["t16"], jnp.where(couple, ch["a"], 0.0).astype(BF16))
        for ch in chains:
            ch["t"] = ch["t"] - _dot(ch["te"].astype(BF16), ch["t16"])
    for ch in chains:
        ch["x"] = _dot(ch["t"].astype(BF16), ch["x"].astype(BF16))

    for ch in chains:
        x16 = ch["x"].astype(BF16)
        ix = _dot(ch["intra16"], x16)
        kx = _dot_tn(ch["ktail16"], x16)
        ch["o_intra"] = ix[:, :HEAD_DIM]
        ch["qeff16"] = (ch["qdec"] - ix[:, HEAD_DIM:]).astype(BF16)
        ch["b"] = kx[:, :HEAD_DIM]
        ch["p16"] = kx[:, HEAD_DIM:].astype(BF16)

    state = [st[h] for h in range(HEADS)]
    for cc in range(nch):
        for ch in chains[cc * HEADS:(cc + 1) * HEADS]:
            h, rs, sl = ch["h"], ch["rs"], ch["sl"]
            s16 = state[h].astype(BF16)
            o = _dot(ch["qeff16"], s16) + ch["o_intra"]
            state[h] = state[h] * ch["gblk"] - _dot(ch["p16"], s16) + ch["b"]
            o = o * lax.rsqrt(jnp.mean(o * o, axis=-1, keepdims=True) + EPS) * dn
            o_ref[rs, sl] = (o * _silu(z_ref[rs, sl])).astype(BF16)
    for h in range(HEADS):
        st[h] = state[h]

    @pl.when(c == nc - 1)
    def _():
        sout_ref[...] = st[...]
        for p in range(3):
            tail_ref[:, p * WIDTH:(p + 1) * WIDTH] = xbuf[p, rows:rows + 8, :]


def _delta(p3, gt3, conv_w, al_row, dt_row, dn_row, conv0, s0):
    b, t, _ = p3.shape
    has_state = s0 is not None
    nch = min(MIXER_CHUNKS_PER_STEP, t // CHUNK)
    rows = nch * CHUNK
    assert t % rows == 0
    col = lambda blk: pl.BlockSpec((None, rows, WIDTH), lambda i, c: (i, c, blk))
    full = lambda shape: pl.BlockSpec(shape, lambda i, c: (0,) * len(shape))
    in_specs = [col(4), col(5), col(6), col(7),
                pl.BlockSpec((None, rows, GATE_LANES), lambda i, c: (i, c, 0)),
                full((CONV_W, 3 * WIDTH)), full((1, GATE_LANES)), full((1, GATE_LANES)),
                full((1, HEAD_DIM))]
    args = [p3, p3, p3, p3, gt3, conv_w, al_row, dt_row, dn_row]
    if has_state:
        in_specs += [pl.BlockSpec((None, 8, 3 * WIDTH), lambda i, c: (i, 0, 0)),
                     pl.BlockSpec((None, HEADS, HEAD_DIM, HEAD_DIM), lambda i, c: (i, 0, 0, 0))]
        args += [conv0, s0]
    return pl.pallas_call(
        functools.partial(_delta_kernel, has_state=has_state, nch=nch),
        grid=(b, t // rows),
        in_specs=in_specs,
        out_specs=[pl.BlockSpec((None, rows, WIDTH), lambda i, c: (i, c, 0)),
                   pl.BlockSpec((None, HEADS, HEAD_DIM, HEAD_DIM), lambda i, c: (i, 0, 0, 0)),
                   pl.BlockSpec((None, 8, 3 * WIDTH), lambda i, c: (i, 0, 0))],
        out_shape=[jax.ShapeDtypeStruct((b, t, WIDTH), BF16),
                   jax.ShapeDtypeStruct((b, HEADS, HEAD_DIM, HEAD_DIM), F32),
                   jax.ShapeDtypeStruct((b, 8, 3 * WIDTH), F32)],
        scratch_shapes=[pltpu.VMEM((3, rows + 8, WIDTH), F32),
                        pltpu.VMEM((3, rows, WIDTH), F32),
                        pltpu.VMEM((HEADS, HEAD_DIM, HEAD_DIM), F32)],
        compiler_params=pltpu.CompilerParams(
            dimension_semantics=("arbitrary", "arbitrary"), vmem_limit_bytes=V7X_VMEM_LIMIT),
        name="delta",
    )(*args)


def _ret_kernel(*refs, has_state, nch):
    if has_state:
        (q_ref, k_ref, v_ref, z_ref, cc_ref, ss_ref, dm_ref, xi_ref, zeta_ref, gb_ref, rn_ref,
         s0_ref, o_ref, sout_ref, st) = refs
    else:
        (q_ref, k_ref, v_ref, z_ref, cc_ref, ss_ref, dm_ref, xi_ref, zeta_ref, gb_ref, rn_ref,
         o_ref, sout_ref, st) = refs
    c = pl.program_id(1)
    nc = pl.num_programs(1)

    @pl.when(c == 0)
    def _():
        if has_state:
            st[...] = s0_ref[...]
        else:
            st[...] = jnp.zeros_like(st)

    chains = []
    for cc in range(nch):
        rs = slice(cc * CHUNK, (cc + 1) * CHUNK)
        cos2 = cc_ref[rs, :]
        sin2 = ss_ref[rs, :]
        for h in range(HEADS):
            sl = slice(h * HEAD_DIM, (h + 1) * HEAD_DIM)
            q = q_ref[rs, sl]
            k = k_ref[rs, sl]
            qh = q * cos2 + pltpu.roll(q, HEAD_DIM // 2, axis=1) * sin2
            kh = (k * cos2 + pltpu.roll(k, HEAD_DIM // 2, axis=1) * sin2) * SCALE
            chains.append(dict(rs=rs, sl=sl, h=h, q16=qh.astype(BF16), k16=kh.astype(BF16),
                               kz16=(kh * zeta_ref[h]).astype(BF16),
                               v16=v_ref[rs, sl].astype(BF16)))
    for ch in chains:
        ch["intra16"] = (_dot_nt(ch["q16"], ch["k16"]) * dm_ref[ch["h"]]).astype(BF16)
    for ch in chains:
        ch["kv"] = _dot_tn(ch["kz16"], ch["v16"])
    for ch in chains:
        ch["o_intra"] = _dot(ch["intra16"], ch["v16"])

    state = [st[h] for h in range(HEADS)]
    for cc in range(nch):
        for ch in chains[cc * HEADS:(cc + 1) * HEADS]:
            h, rs, sl = ch["h"], ch["rs"], ch["sl"]
            ch["o"] = ch["o_intra"] + _dot(ch["q16"], state[h].astype(BF16)) * xi_ref[h]
            state[h] = state[h] * gb_ref[h] + ch["kv"]
    for h in range(HEADS):
        st[h] = state[h]

    for ch in chains:
        ch["d"] = ch["o"] - jnp.mean(ch["o"], axis=-1, keepdims=True)
    for ch in chains:
        ch["var"] = jnp.mean(ch["d"] * ch["d"], axis=-1, keepdims=True)
    for ch in chains:
        rs, sl = ch["rs"], ch["sl"]
        o = ch["d"] * lax.rsqrt(ch["var"] + EPS) * rn_ref[:, sl]
        o_ref[rs, sl] = (o * _silu(z_ref[rs, sl])).astype(BF16)

    @pl.when(c == nc - 1)
    def _():
        sout_ref[...] = st[...]


def _ret(p3, cc, ss, consts, rn_row, s0):
    b, t, _ = p3.shape
    has_state = s0 is not None
    nch = min(MIXER_CHUNKS_PER_STEP, t // CHUNK)
    rows = nch * CHUNK
    assert t % rows == 0
    dmat, xi, zeta, gblk = consts
    col = lambda blk: pl.BlockSpec((None, rows, WIDTH), lambda i, c: (i, c, blk))
    full = lambda shape: pl.BlockSpec(shape, lambda i, c: (0,) * len(shape))
    tab = pl.BlockSpec((rows, HEAD_DIM), lambda i, c: (c, 0))
    in_specs = [col(8), col(9), col(10), col(11), tab, tab,
                full((HEADS, CHUNK, CHUNK)), full((HEADS, CHUNK, HEAD_DIM)),
                full((HEADS, CHUNK, HEAD_DIM)), full((HEADS, 1, HEAD_DIM)), full((1, WIDTH))]
    args = [p3, p3, p3, p3, cc, ss, dmat, xi, zeta, gblk, rn_row]
    if has_state:
        in_specs.append(pl.BlockSpec((None, HEADS, HEAD_DIM, HEAD_DIM), lambda i, c: (i, 0, 0, 0)))
        args.append(s0)
    return pl.pallas_call(
        functools.partial(_ret_kernel, has_state=has_state, nch=nch),
        grid=(b, t // rows),
        in_specs=in_specs,
        out_specs=[pl.BlockSpec((None, rows, WIDTH), lambda i, c: (i, c, 0)),
                   pl.BlockSpec((None, HEADS, HEAD_DIM, HEAD_DIM), lambda i, c: (i, 0, 0, 0))],
        out_shape=[jax.ShapeDtypeStruct((b, t, WIDTH), BF16),
                   jax.ShapeDtypeStruct((b, HEADS, HEAD_DIM, HEAD_DIM), F32)],
        scratch_shapes=[pltpu.VMEM((HEADS, HEAD_DIM, HEAD_DIM), F32)],
        compiler_params=pltpu.CompilerParams(
            dimension_semantics=("arbitrary", "arbitrary"), vmem_limit_bytes=V7X_VMEM_LIMIT),
        name="ret",
    )(*args)


def _merge_kernel(oa_ref, ob_ref, oc_ref, ga_ref, gb_ref, gc_ref, wa_ref, wb_ref, wc_ref,
                  wo_ref, x_ref, g_ref, o_ref):
    merged = (jax.nn.sigmoid(ga_ref[...]) * _dot(oa_ref[...], wa_ref[...])
              + jax.nn.sigmoid(gb_ref[...]) * _dot(ob_ref[...], wb_ref[...])
              + jax.nn.sigmoid(gc_ref[...]) * _dot(oc_ref[...], wc_ref[...]))
    y = _dot(merged.astype(BF16), wo_ref[...])
    y = y * lax.rsqrt(jnp.mean(y * y, axis=-1, keepdims=True) + EPS) * g_ref[...]
    o_ref[...] = x_ref[...] + y


def _merge(oa, ob, oc, p2, x2d, wa, wb, wc, wo, g):
    m = x2d.shape[0]
    tm = min(m, 256)
    row = lambda w: pl.BlockSpec((tm, w), lambda i: (i, 0))
    gate = lambda blk: pl.BlockSpec((tm, D_MODEL), lambda i: (i, blk))
    const = lambda shape: pl.BlockSpec(shape, lambda i: (0, 0), pipeline_mode=pl.Buffered(1))
    return pl.pallas_call(
        _merge_kernel,
        grid=(m // tm,),
        in_specs=[row(WIDTH), row(WIDTH), row(WIDTH), gate(6), gate(7), gate(8),
                  const((WIDTH, D_MODEL)), const((WIDTH, D_MODEL)), const((WIDTH, D_MODEL)),
                  const((D_MODEL, D_MODEL)), row(D_MODEL), const((1, D_MODEL))],
        out_specs=row(D_MODEL),
        out_shape=jax.ShapeDtypeStruct((m, D_MODEL), F32),
        compiler_params=pltpu.CompilerParams(
            dimension_semantics=("arbitrary",), vmem_limit_bytes=V7X_VMEM_LIMIT),
        name="merge",
    )(oa, ob, oc, p2, p2, p2, wa, wb, wc, wo, x2d, g)


def _retention_constants():
    log_gamma = jnp.log1p(-jnp.exp2(-5.0 - jnp.arange(HEADS, dtype=F32)))
    idx = jnp.arange(CHUNK, dtype=F32)
    rel = idx[:, None] - idx[None, :]
    lg = log_gamma[:, None, None]
    dmat = jnp.where(rel >= 0, jnp.exp(lg * jnp.maximum(rel, 0.0)), 0.0)
    xi = jnp.exp(log_gamma[:, None] * (idx + 1.0))
    zeta = jnp.exp(log_gamma[:, None] * (CHUNK - 1.0 - idx))
    gblk = jnp.exp(log_gamma * CHUNK)
    bc = lambda v: jnp.broadcast_to(v[:, :, None], (HEADS, v.shape[1], HEAD_DIM))
    return dmat, bc(xi), bc(zeta), bc(gblk[:, None])


def _rotary_tables(pos):
    half = HEAD_DIM // 2
    inv = ROPE_BASE ** (-jnp.arange(half, dtype=F32) / half)
    ang = pos.astype(F32)[:, None] * inv[None, :]
    cos, sin = jnp.cos(ang), jnp.sin(ang)
    return jnp.concatenate([cos, cos], axis=1), jnp.concatenate([-sin, sin], axis=1)


def _rev_bias(rel_bias):
    n_off = BAND_KEYS + CHUNK - 1
    head = rel_bias[:, MAX_REL - (CHUNK - 1):].astype(F32)
    edge = jnp.broadcast_to(head[:, -1:], (rel_bias.shape[0], n_off - head.shape[1]))
    ext = jnp.concatenate([head, edge], axis=1)
    return jnp.pad(ext[:, ::-1], ((0, 0), (0, BIAS_LANES - n_off)))


def _lane_row(v, lane0):
    return jnp.zeros((1, GATE_LANES), F32).at[0, lane0:lane0 + v.shape[0]].set(v.astype(F32))


def _group_layer(x, pos, cache, conv_buf, s_delta, s_ret, lw, ret_consts):
    (norm_pre, norm_post, w_main, w_gate, bias, conv_w, al_row, dt_row, dn_row, rn_row,
     wa, wb, wc, wo) = lw
    b, t, _ = x.shape
    x2d = x.reshape(b * t, D_MODEL)
    p2, gt2 = _inproj(x2d, norm_pre, w_main, w_gate)
    p3 = p2.reshape(b, t, N_MAIN)
    gt3 = gt2.reshape(b, t, GATE_LANES)

    oa, new_k, new_v = _attn(p3, bias, cache)

    conv0 = None if conv_buf is None else jnp.pad(conv_buf, ((0, 0), (8 - (CONV_W - 1), 0), (0, 0)))
    ob, new_sd, conv_tail = _delta(p3, gt3, conv_w, al_row, dt_row, dn_row, conv0, s_delta)
    new_conv = conv_tail[:, 8 - (CONV_W - 1):, :]

    cc, ss = _rotary_tables(pos)
    oc, new_sr = _ret(p3, cc, ss, ret_consts, rn_row, s_ret)

    y2d = _merge(oa.reshape(b * t, WIDTH), ob.reshape(b * t, WIDTH), oc.reshape(b * t, WIDTH),
                 p2, x2d, wa, wb, wc, wo, norm_post)
    return y2d.reshape(b, t, D_MODEL), new_k, new_v, new_conv, new_sd, new_sr


def kernel(x_prompt, x_sample, cache_attn_k, cache_attn_v, state_conv, state_delta, state_ret,
           norm_pre, norm_post, w_in, attn_rel_bias, conv_w, delta_a_log, delta_dt_bias,
           delta_norm, ret_norm, w_branch_a, w_branch_b, w_branch_c, w_out):
    depth = w_in.shape[0]
    tp, ts = x_prompt.shape[1], x_sample.shape[1]
    pos_p = jnp.arange(tp, dtype=jnp.int32)
    pos_s = PAST_LEN + jnp.arange(ts, dtype=jnp.int32)
    ret_consts = _retention_constants()
    w_in_t = jnp.swapaxes(w_in, 1, 2)
    xp, xs = x_prompt, x_sample
    acc_p = [[] for _ in range(5)]
    acc_s = [[] for _ in range(5)]
    for l in range(depth):
        w_main, w_gate = _weight_prep(w_in_t, l)
        lw =(norm_pre[l][None, :], norm_post[l][None, :], w_main, w_gate,
              _rev_bias(attn_rel_bias[l]), conv_w[l],
              _lane_row(delta_a_log[l], HEADS), _lane_row(delta_dt_bias[l], HEADS),
              delta_norm[l][None, :].astype(F32), ret_norm[l].reshape(1, WIDTH).astype(F32),
              w_branch_a[l].astype(BF16), w_branch_b[l].astype(BF16), w_branch_c[l].astype(BF16),
              w_out[l].astype(BF16))
        xp, *st_p = _group_layer(xp, pos_p, None, None, None, None, lw, ret_consts)
        xs, *st_s = _group_layer(xs, pos_s, (cache_attn_k, cache_attn_v, l), state_conv[l],
                                 state_delta[l], state_ret[l], lw, ret_consts)
        for acc, val in zip(acc_p, st_p):
            acc.append(val)
        for acc, val in zip(acc_s, st_s):
            acc.append(val)
    outs_p = [jnp.stack(a, axis=0) for a in acc_p]
    outs_s = [jnp.stack(a, axis=0) for a in acc_s]
    return (xp, xs, *outs_p, *outs_s)
```

```python
import functools

import jax
import jax.numpy as jnp
import numpy as np
from jax import lax
from jax.experimental import pallas as pl
from jax.experimental.pallas import tpu as pltpu

F32 = jnp.float32
BF16 = jnp.bfloat16

D_MODEL = 2048
CHUNK = 64
HEAD_DIM = 128
HEADS = 8
WIDTH = HEADS * HEAD_DIM
BAND_CHUNKS = 8
BAND_ROWS = BAND_CHUNKS * CHUNK
BAND_KEYS = BAND_ROWS + CHUNK
BIAS_LANES = 640
MAX_REL = 128
CONV_W = 4
PAST_LEN = 2048
ROPE_BASE = 10000.0
EPS = 1e-6
NEG_INF = -1e30
SCALE = HEAD_DIM ** -0.5
LOG2E = 1.4426950408889634

N_MAIN = 18 * WIDTH
GATE_COL0 = 8 * WIDTH
GATE_LANES = 128
V7X_VMEM_LIMIT = 60 * 1024 * 1024
MIXER_CHUNKS_PER_STEP = 4
ATTN_GROUP_CHUNKS = 2

_NT = (((1,), (1,)), ((), ()))
_TN = (((0,), (0,)), ((), ()))


def _dot(a, b):
    return jnp.dot(a, b, preferred_element_type=F32)


def _dot_nt(a, b):
    return lax.dot_general(a, b, _NT, preferred_element_type=F32)


def _dot_tn(a, b):
    return lax.dot_general(a, b, _TN, preferred_element_type=F32)


def _silu(x):
    return x * jax.nn.sigmoid(x)


def _softplus(x):
    return jnp.maximum(x, 0.0) + jnp.log1p(jnp.exp(-jnp.abs(x)))


PREP_COLS = 512
N_GATE = 2 * HEADS
assert GATE_COL0 % PREP_COLS == 0 and N_GATE % 8 == 0 and N_GATE <= GATE_LANES


def _weight_prep_kernel(a_ref, b_ref, o_ref, og_ref):
    n = pl.program_id(0)
    first_shifted = GATE_COL0 // PREP_COLS

    @pl.when(n < first_shifted)
    def _():
        o_ref[...] = jnp.transpose(a_ref[...]).astype(BF16)

    @pl.when(n >= first_shifted)
    def _():
        src = jnp.concatenate([a_ref[N_GATE:PREP_COLS, :], b_ref[...]], axis=0)
        o_ref[...] = jnp.transpose(src).astype(BF16)

    @pl.when(n == first_shifted)
    def _():
        pad = jnp.zeros((GATE_LANES - N_GATE, D_MODEL), F32)
        og_ref[...] = jnp.concatenate([a_ref[0:N_GATE, :], pad], axis=0).astype(BF16)


def _weight_prep(w_in_t, layer):
    assert w_in_t.shape[1] == N_MAIN + N_GATE
    return pl.pallas_call(
        _weight_prep_kernel,
        grid=(N_MAIN // PREP_COLS,),
        in_specs=[pl.BlockSpec((None, PREP_COLS, D_MODEL), lambda n: (layer, n, 0)),
                  pl.BlockSpec((None, N_GATE, D_MODEL),
                               lambda n: (layer, (n + 1) * (PREP_COLS // N_GATE), 0))],
        out_specs=[pl.BlockSpec((D_MODEL, PREP_COLS), lambda n: (0, n)),
                   pl.BlockSpec((GATE_LANES, D_MODEL), lambda n: (0, 0))],
        out_shape=[jax.ShapeDtypeStruct((D_MODEL, N_MAIN), BF16),
                   jax.ShapeDtypeStruct((GATE_LANES, D_MODEL), BF16)],
        compiler_params=pltpu.CompilerParams(
            dimension_semantics=("arbitrary",), vmem_limit_bytes=V7X_VMEM_LIMIT),
        name="wprep",
    )(w_in_t, w_in_t)


def _inproj_kernel(x_ref, g_ref, w_ref, wg_ref, o_ref, og_ref, h_ref):
    @pl.when(pl.program_id(1) == 0)
    def _():
        x = x_ref[...]
        ms = jnp.mean(x * x, axis=-1, keepdims=True)
        h = (x * lax.rsqrt(ms + EPS) * g_ref[...]).astype(BF16)
        h_ref[...] = h
        og_ref[...] = _dot_nt(h, wg_ref[...])

    o_ref[...] = _dot(h_ref[...], w_ref[...])


def _inproj(x2d, g, w_main, w_gate):
    m = x2d.shape[0]
    tm = min(m, 1024)
    tn = 2048
    return pl.pallas_call(
        _inproj_kernel,
        grid=(m // tm, N_MAIN // tn),
        in_specs=[
            pl.BlockSpec((tm, D_MODEL), lambda i, n: (i, 0)),
            pl.BlockSpec((1, D_MODEL), lambda i, n: (0, 0)),
            pl.BlockSpec((D_MODEL, tn), lambda i, n: (0, n)),
            pl.BlockSpec((GATE_LANES, D_MODEL), lambda i, n: (0, 0)),
        ],
        out_specs=[
            pl.BlockSpec((tm, tn), lambda i, n: (i, n)),
            pl.BlockSpec((tm, GATE_LANES), lambda i, n: (i, 0)),
        ],
        out_shape=[
            jax.ShapeDtypeStruct((m, N_MAIN), F32),
            jax.ShapeDtypeStruct((m, GATE_LANES), F32),
        ],
        scratch_shapes=[pltpu.VMEM((tm, D_MODEL), BF16)],
        compiler_params=pltpu.CompilerParams(
            dimension_semantics=("arbitrary", "arbitrary"),
            vmem_limit_bytes=V7X_VMEM_LIMIT),
        name="inproj",
    )(x2d, g, w_main, w_gate)


def _attn_kernel(*refs, nq, gq, has_cache):
    if has_cache:
        (q_ref, ko_ref, vo_ref, z_ref, rb_ref, kc_ref, vc_ref,
         o_ref, nk_ref, nv_ref, kt, vcat, bias_scr) = refs
    else:
        (q_ref, ko_ref, vo_ref, z_ref, rb_ref,
         o_ref, nk_ref, nv_ref, kt, vcat, bias_scr) = refs
    t = pl.program_id(1)
    nt = pl.num_programs(1)
    rows = nq * CHUNK
    grows = gq * CHUNK
    gw = BAND_ROWS + grows

    @pl.when(jnp.logical_and(pl.program_id(0) == 0, t == 0))
    def _():
        lane = lax.broadcasted_iota(jnp.int32, (CHUNK, BIAS_LANES), 1)
        for h in range(HEADS):
            wide = jnp.broadcast_to(rb_ref[h:h + 1, :], (CHUNK, BIAS_LANES))
            for e in range(gq):
                shift = (BIAS_LANES - (CHUNK - 1) + e * CHUNK) % BIAS_LANES
                tab = pltpu.roll(wide, shift, 1, stride=1, stride_axis=0) * LOG2E
                tab = jnp.where(lane < e * CHUNK, NEG_INF, tab)
                tab = jnp.where(lane >= e * CHUNK + BAND_KEYS, NEG_INF, tab)
                bias_scr[h, e * CHUNK:(e + 1) * CHUNK, :] = tab

    eye16 = (lax.broadcasted_iota(jnp.int32, (HEAD_DIM, HEAD_DIM), 0)
             == lax.broadcasted_iota(jnp.int32, (HEAD_DIM, HEAD_DIM), 1)).astype(BF16)

    def transpose16(x16):
        return _dot_nt(eye16, x16).astype(BF16)

    if has_cache:
        for h in range(HEADS):
            sl = slice(h * HEAD_DIM, (h + 1) * HEAD_DIM)
            kt[h, :, 0:BAND_ROWS] = transpose16(kc_ref[:, h, :].astype(BF16))
            vcat[0:BAND_ROWS, sl] = vc_ref[:, h, :].astype(BF16)
    else:
        @pl.when(t == 0)
        def _():
            kt[:, :, 0:BAND_ROWS] = jnp.zeros((HEADS, HEAD_DIM, BAND_ROWS), BF16)
            vcat[0:BAND_ROWS, :] = jnp.zeros((BAND_ROWS, WIDTH), BF16)

        @pl.when(t > 0)
        def _():
            kt[:, :, 0:BAND_ROWS] = kt[:, :, rows:rows + BAND_ROWS]
            vcat[0:BAND_ROWS, :] = vcat[rows:rows + BAND_ROWS, :]
    for h in range(HEADS):
        sl = slice(h * HEAD_DIM, (h + 1) * HEAD_DIM)
        kt[h, :, BAND_ROWS:BAND_ROWS + rows] = transpose16(ko_ref[:, sl].astype(BF16))
    vcat[BAND_ROWS:BAND_ROWS + rows, :] = vo_ref[...].astype(BF16)

    col = lax.broadcasted_iota(jnp.int32, (grows, gw), 1)
    for h in range(HEADS):
        sl = slice(h * HEAD_DIM, (h + 1) * HEAD_DIM)
        bias = bias_scr[h][:, :gw]
        scores = []
        for g in range(nq // gq):
            r0 = g * grows
            q = (q_ref[r0:r0 + grows, sl] * (SCALE * LOG2E)).astype(BF16)
            scores.append(_dot(q, kt[h, :, r0:r0 + gw]))
        probs = []
        for g in range(nq // gq):
            s = scores[g] + bias
            if not has_cache:
                first_valid = jnp.where(t == 0, BAND_ROWS - g * grows, 0)
                s = jnp.where(col < first_valid, NEG_INF, s)
            m = jnp.max(s, axis=-1, keepdims=True)
            p = jnp.exp2(s - m)
            probs.append((p.astype(BF16), jnp.sum(p, axis=-1, keepdims=True)))
        outs = []
        for g in range(nq // gq):
            r0 = g * grows
            outs.append(_dot(probs[g][0], vcat[r0:r0 + gw, sl]))
        for g in range(nq // gq):
            r0 = g * grows
            o = outs[g] / probs[g][1]
            o_ref[r0:r0 + grows, sl] = (o * _silu(z_ref[r0:r0 + grows, sl])).astype(BF16)

    @pl.when(t == nt - 1)
    def _():
        if rows < BAND_ROWS:
            nk_ref[0:BAND_ROWS - rows] = kc_ref[rows:BAND_ROWS]
            nv_ref[0:BAND_ROWS - rows] = vc_ref[rows:BAND_ROWS]
        for h in range(HEADS):
            sl = slice(h * HEAD_DIM, (h + 1) * HEAD_DIM)
            nk_ref[BAND_ROWS - rows:BAND_ROWS, h, :] = ko_ref[:, sl]
            nv_ref[BAND_ROWS - rows:BAND_ROWS, h, :] = vo_ref[:, sl]


def _attn(p3, rev_bias, cache):
    b, t, _ = p3.shape
    has_cache = cache is not None
    rows = CHUNK if has_cache else BAND_ROWS
    assert t % rows == 0 and (not has_cache or t == CHUNK)
    nq = rows // CHUNK
    gq = min(nq, ATTN_GROUP_CHUNKS)
    assert BAND_ROWS + gq * CHUNK <= BIAS_LANES
    cur = lambda blk: pl.BlockSpec((None, rows, WIDTH), lambda i, tt: (i, tt, blk))
    newest =pl.BlockSpec((None, BAND_ROWS, HEADS, HEAD_DIM), lambda i, tt: (i, 0, 0, 0))
    in_specs = [cur(0), cur(1), cur(2), cur(3),
                pl.BlockSpec((HEADS, BIAS_LANES), lambda i, tt: (0, 0))]
    args = [p3, p3, p3, p3, rev_bias]
    if has_cache:
        ck, cv, layer = cache
        assert ck.shape[2:] == (BAND_ROWS, HEADS, HEAD_DIM)
        cached = pl.BlockSpec((None, None, BAND_ROWS, HEADS, HEAD_DIM),
                              lambda i, tt: (layer, i, 0, 0, 0))
        in_specs += [cached, cached]
        args += [ck, cv]
    return pl.pallas_call(
        functools.partial(_attn_kernel, nq=nq, gq=gq, has_cache=has_cache),
        grid=(b, t // rows),
        in_specs=in_specs,
        out_specs=[cur(0), newest, newest],
        out_shape=[jax.ShapeDtypeStruct((b, t, WIDTH), BF16),
                   jax.ShapeDtypeStruct((b, BAND_ROWS, HEADS, HEAD_DIM), F32),
                   jax.ShapeDtypeStruct((b, BAND_ROWS, HEADS, HEAD_DIM), F32)],
        scratch_shapes=[pltpu.VMEM((HEADS, HEAD_DIM, BAND_ROWS + rows), BF16),
                        pltpu.VMEM((BAND_ROWS + rows, WIDTH), BF16),
                        pltpu.VMEM((HEADS, gq * CHUNK, BIAS_LANES), F32)],
        compiler_params=pltpu.CompilerParams(
            dimension_semantics=("arbitrary", "arbitrary"), vmem_limit_bytes=V7X_VMEM_LIMIT),
        name="attn",
    )(*args)


def _delta_kernel(*refs, has_state, nch):
    if has_state:
        (q_ref, k_ref, v_ref, z_ref, gt_ref, cw_ref, al_ref, dt_ref, dn_ref, c0_ref, s0_ref,
         o_ref, sout_ref, tail_ref, xbuf, act, st) = refs
    else:
        (q_ref, k_ref, v_ref, z_ref, gt_ref, cw_ref, al_ref, dt_ref, dn_ref,
         o_ref, sout_ref, tail_ref, xbuf, act, st) = refs
    c = pl.program_id(1)
    nc = pl.num_programs(1)
    rows = nch * CHUNK

    @pl.when(c == 0)
    def _():
        if has_state:
            for p in range(3):
                xbuf[p, 0:8, :] = c0_ref[:, p * WIDTH:(p + 1) * WIDTH]
            st[...] = s0_ref[...]
        else:
            xbuf[:, 0:8, :] = jnp.zeros((3, 8, WIDTH), F32)
            st[...] = jnp.zeros_like(st)

    @pl.when(c > 0)
    def _():
        xbuf[:, 0:8, :] = xbuf[:, rows:rows + 8, :]

    xbuf[0, 8:8 + rows, :] = q_ref[...]
    xbuf[1, 8:8 + rows, :] = k_ref[...]
    xbuf[2, 8:8 + rows, :] = v_ref[...]

    for p in range(3):
        y = None
        for i in range(CONV_W):
            start = 8 - (CONV_W - 1) + i
            term = xbuf[p, start:start + rows, :] * cw_ref[i:i + 1, p * WIDTH:(p + 1) * WIDTH]
            y = term if y is None else y + term
        act[p] = _silu(y)

    ri = lax.broadcasted_iota(jnp.int32, (CHUNK, CHUNK), 0)
    ci = lax.broadcasted_iota(jnp.int32, (CHUNK, CHUNK), 1)
    causal = ri >= ci
    strict = ri > ci
    ltri = causal.astype(F32)
    dn = dn_ref[...]

    chains = []
    for cc in range(nch):
        rs = slice(cc * CHUNK, (cc + 1) * CHUNK)
        gt = gt_ref[rs, :]
        beta_all = jax.nn.sigmoid(gt)
        g_all = -jnp.exp(al_ref[...]) * _softplus(gt + dt_ref[...])
        gcum = jnp.dot(ltri, g_all, precision=lax.Precision.HIGHEST,
                       preferred_element_type=F32)
        gcum_t = jnp.transpose(gcum)
        eg_all = jnp.exp(gcum)
        glast = gcum[CHUNK - 1:CHUNK, :]
        tail_all = jnp.exp(glast - gcum)
        gblk_all = jnp.exp(glast)
        for h in range(HEADS):
            sl = slice(h * HEAD_DIM, (h + 1) * HEAD_DIM)
            qh, kh, vh = act[0, rs, sl], act[1, rs, sl], act[2, rs, sl]
            qh = qh * lax.rsqrt(jnp.sum(qh * qh, axis=-1, keepdims=True) + EPS)
            kh = kh * lax.rsqrt(jnp.sum(kh * kh, axis=-1, keepdims=True) + EPS)
            beta = beta_all[:, h:h + 1]
            gl = HEADS + h
            eg = eg_all[:, gl:gl + 1]
            diff = gcum[:, gl:gl + 1] - gcum_t[gl:gl + 1, :]
            kbeta = kh * beta
            qs = qh * SCALE
            chains.append(dict(
                rs=rs, sl=sl, h=h,
                decay=jnp.exp(jnp.where(causal, diff, -jnp.inf)),
                k16=kh.astype(BF16), kbeta16=kbeta.astype(BF16), qs16=qs.astype(BF16),
                qdec=qs * eg,
                ktail16=(kh * tail_all[:, gl:gl + 1]).astype(BF16),
                gblk=gblk_all[:, gl:gl + 1],
                x=jnp.concatenate([vh * beta, kbeta * eg], axis=1)))

    for ch in chains:
        ch["a"] = jnp.where(strict, _dot_nt(ch["kbeta16"], ch["k16"]) * ch["decay"], 0.0)
    for ch in chains:
        ch["intra16"] = (_dot_nt(ch["qs16"], ch["k16"]) * ch["decay"]).astype(BF16)

    def same_block(size):
        shift = size.bit_length() - 1
        return jnp.right_shift(ri, shift) == jnp.right_shift(ci, shift)

    eye = (ri == ci).astype(F32)
    for ch in chains:
        n1 = jnp.where(same_block(8), -ch["a"], 0.0)
        ch["n1_16"] = n1.astype(BF16)
        ch["p"] = eye + n1
    for ch in chains:
        ch["n2"] = _dot(ch["n1_16"], ch["n1_16"])
    for ch in chains:
        ch["n2_16"] = ch["n2"].astype(BF16)
        ch["n4"] = _dot(ch["n2_16"], ch["n2_16"])
        ch["p"] = ch["p"] + _dot(ch["p"].astype(BF16), ch["n2_16"])
    for ch in chains:
        ch["t"] = ch["p"] + _dot(ch["p"].astype(BF16), ch["n4"].astype(BF16))
    for size in (8, 16, 32):
        couple = jnp.logical_and(same_block(2 * size), jnp.logical_not(same_block(size)))
        for ch in chains:
            ch["t16"] = ch["t"].astype(BF16)
            ch["te"] = _dot(ch["t16"], jnp.where(couple, ch["a"], 0.0).astype(BF16))
        for ch in chains:
            ch["t"] = ch["t"] - _dot(ch["te"].astype(BF16), ch["t16"])
    for ch in chains:
        ch["x"] = _dot(ch["t"].astype(BF16), ch["x"].astype(BF16))

    for ch in chains:
        x16 = ch["x"].astype(BF16)
        ix = _dot(ch["intra16"], x16)
        kx = _dot_tn(ch["ktail16"], x16)
        ch["o_intra"] = ix[:, :HEAD_DIM]
        ch["qeff16"] = (ch["qdec"] - ix[:, HEAD_DIM:]).astype(BF16)
        ch["b"] = kx[:, :HEAD_DIM]
        ch["p16"] = kx[:, HEAD_DIM:].astype(BF16)

    state = [st[h] for h in range(HEADS)]
    for cc in range(nch):
        for ch in chains[cc * HEADS:(cc + 1) * HEADS]:
            h, rs, sl = ch["h"], ch["rs"], ch["sl"]
            s16 = state[h].astype(BF16)
            o = _dot(ch["qeff16"], s16) + ch["o_intra"]
            state[h] = state[h] * ch["gblk"] - _dot(ch["p16"], s16) + ch["b"]
            o = o * lax.rsqrt(jnp.mean(o * o, axis=-1, keepdims=True) + EPS) * dn
            o_ref[rs, sl] = (o * _silu(z_ref[rs, sl])).astype(BF16)
    for h in range(HEADS):
        st[h] = state[h]

    @pl.when(c == nc - 1)
    def _():
        sout_ref[...] = st[...]
        for p in range(3):
            tail_ref[:, p * WIDTH:(p + 1) * WIDTH] = xbuf[p, rows:rows + 8, :]


def _delta(p3, gt3, conv_w, al_row, dt_row, dn_row, conv0, s0):
    b, t, _ = p3.shape
    has_state = s0 is not None
    nch = min(MIXER_CHUNKS_PER_STEP, t // CHUNK)
    rows = nch * CHUNK
    assert t % rows == 0
    col = lambda blk: pl.BlockSpec((None, rows, WIDTH), lambda i, c: (i, c, blk))
    full = lambda shape: pl.BlockSpec(shape, lambda i, c: (0,) * len(shape))
    in_specs = [col(4), col(5), col(6), col(7),
                pl.BlockSpec((None, rows, GATE_LANES), lambda i, c: (i, c, 0)),
                full((CONV_W, 3 * WIDTH)), full((1, GATE_LANES)), full((1, GATE_LANES)),
                full((1, HEAD_DIM))]
    args = [p3, p3, p3, p3, gt3, conv_w, al_row, dt_row, dn_row]
    if has_state:
        in_specs += [pl.BlockSpec((None, 8, 3 * WIDTH), lambda i, c: (i, 0, 0)),
                     pl.BlockSpec((None, HEADS, HEAD_DIM, HEAD_DIM), lambda i, c: (i, 0, 0, 0))]
        args += [conv0, s0]
    return pl.pallas_call(
        functools.partial(_delta_kernel, has_state=has_state, nch=nch),
        grid=(b, t // rows),
        in_specs=in_specs,
        out_specs=[pl.BlockSpec((None, rows, WIDTH), lambda i, c: (i, c, 0)),
                   pl.BlockSpec((None, HEADS, HEAD_DIM, HEAD_DIM), lambda i, c: (i, 0, 0, 0)),
                   pl.BlockSpec((None, 8, 3 * WIDTH), lambda i, c: (i, 0, 0))],
        out_shape=[jax.ShapeDtypeStruct((b, t, WIDTH), BF16),
                   jax.ShapeDtypeStruct((b, HEADS, HEAD_DIM, HEAD_DIM), F32),
                   jax.ShapeDtypeStruct((b, 8, 3 * WIDTH), F32)],
        scratch_shapes=[pltpu.VMEM((3, rows + 8, WIDTH), F32),
                        pltpu.VMEM((3, rows, WIDTH), F32),
                        pltpu.VMEM((HEADS, HEAD_DIM, HEAD_DIM), F32)],
        compiler_params=pltpu.CompilerParams(
            dimension_semantics=("arbitrary", "arbitrary"), vmem_limit_bytes=V7X_VMEM_LIMIT),
        name="delta",
    )(*args)


def _ret_kernel(*refs, has_state, nch):
    if has_state:
        (q_ref, k_ref, v_ref, z_ref, cc_ref, ss_ref, dm_ref, xi_ref, zeta_ref, gb_ref, rn_ref,
         s0_ref, o_ref, sout_ref, st) = refs
    else:
        (q_ref, k_ref, v_ref, z_ref, cc_ref, ss_ref, dm_ref, xi_ref, zeta_ref, gb_ref, rn_ref,
         o_ref, sout_ref, st) = refs
    c = pl.program_id(1)
    nc = pl.num_programs(1)

    @pl.when(c == 0)
    def _():
        if has_state:
            st[...] = s0_ref[...]
        else:
            st[...] = jnp.zeros_like(st)

    chains = []
    for cc in range(nch):
        rs = slice(cc * CHUNK, (cc + 1) * CHUNK)
        cos2 = cc_ref[rs, :]
        sin2 = ss_ref[rs, :]
        for h in range(HEADS):
            sl = slice(h * HEAD_DIM, (h + 1) * HEAD_DIM)
            q = q_ref[rs, sl]
            k = k_ref[rs, sl]
            qh = q * cos2 + pltpu.roll(q, HEAD_DIM // 2, axis=1) * sin2
            kh = (k * cos2 + pltpu.roll(k, HEAD_DIM // 2, axis=1) * sin2) * SCALE
            chains.append(dict(rs=rs, sl=sl, h=h, q16=qh.astype(BF16), k16=kh.astype(BF16),
                               kz16=(kh * zeta_ref[h]).astype(BF16),
                               v16=v_ref[rs, sl].astype(BF16)))
    for ch in chains:
        ch["intra16"] = (_dot_nt(ch["q16"], ch["k16"]) * dm_ref[ch["h"]]).astype(BF16)
    for ch in chains:
        ch["kv"] = _dot_tn(ch["kz16"], ch["v16"])
    for ch in chains:
        ch["o_intra"] = _dot(ch["intra16"], ch["v16"])

    state = [st[h] for h in range(HEADS)]
    for cc in range(nch):
        for ch in chains[cc * HEADS:(cc + 1) * HEADS]:
            h, rs, sl = ch["h"], ch["rs"], ch["sl"]
            ch["o"] = ch["o_intra"] + _dot(ch["q16"], state[h].astype(BF16)) * xi_ref[h]
            state[h] = state[h] * gb_ref[h] + ch["kv"]
    for h in range(HEADS):
        st[h] = state[h]

    for ch in chains:
        ch["d"] = ch["o"] - jnp.mean(ch["o"], axis=-1, keepdims=True)
    for ch in chains:
        ch["var"] = jnp.mean(ch["d"] * ch["d"], axis=-1, keepdims=True)
    for ch in chains:
        rs, sl = ch["rs"], ch["sl"]
        o = ch["d"] * lax.rsqrt(ch["var"] + EPS) * rn_ref[:, sl]
        o_ref[rs, sl] = (o * _silu(z_ref[rs, sl])).astype(BF16)

    @pl.when(c == nc - 1)
    def _():
        sout_ref[...] = st[...]


def _ret(p3, cc, ss, consts, rn_row, s0):
    b, t, _ = p3.shape
    has_state = s0 is not None
    nch = min(MIXER_CHUNKS_PER_STEP, t // CHUNK)
    rows = nch * CHUNK
    assert t % rows == 0
    dmat, xi, zeta, gblk = consts
    col = lambda blk: pl.BlockSpec((None, rows, WIDTH), lambda i, c: (i, c, blk))
    full = lambda shape: pl.BlockSpec(shape, lambda i, c: (0,) * len(shape))
    tab = pl.BlockSpec((rows, HEAD_DIM), lambda i, c: (c, 0))
    in_specs = [col(8), col(9), col(10), col(11), tab, tab,
                full((HEADS, CHUNK, CHUNK)), full((HEADS, CHUNK, HEAD_DIM)),
                full((HEADS, CHUNK, HEAD_DIM)), full((HEADS, 1, HEAD_DIM)), full((1, WIDTH))]
    args = [p3, p3, p3, p3, cc, ss, dmat, xi, zeta, gblk, rn_row]
    if has_state:
        in_specs.append(pl.BlockSpec((None, HEADS, HEAD_DIM, HEAD_DIM), lambda i, c: (i, 0, 0, 0)))
        args.append(s0)
    return pl.pallas_call(
        functools.partial(_ret_kernel, has_state=has_state, nch=nch),
        grid=(b, t // rows),
        in_specs=in_specs,
        out_specs=[pl.BlockSpec((None, rows, WIDTH), lambda i, c: (i, c, 0)),
                   pl.BlockSpec((None, HEADS, HEAD_DIM, HEAD_DIM), lambda i, c: (i, 0, 0, 0))],
        out_shape=[jax.ShapeDtypeStruct((b, t, WIDTH), BF16),
                   jax.ShapeDtypeStruct((b, HEADS, HEAD_DIM, HEAD_DIM), F32)],
        scratch_shapes=[pltpu.VMEM((HEADS, HEAD_DIM, HEAD_DIM), F32)],
        compiler_params=pltpu.CompilerParams(
            dimension_semantics=("arbitrary", "arbitrary"), vmem_limit_bytes=V7X_VMEM_LIMIT),
        name="ret",
    )(*args)


def _merge_kernel(oa_ref, ob_ref, oc_ref, ga_ref, gb_ref, gc_ref, wa_ref, wb_ref, wc_ref,
                  wo_ref, x_ref, g_ref, o_ref):
    merged = (jax.nn.sigmoid(ga_ref[...]) * _dot(oa_ref[...], wa_ref[...])
              + jax.nn.sigmoid(gb_ref[...]) * _dot(ob_ref[...], wb_ref[...])
              + jax.nn.sigmoid(gc_ref[...]) * _dot(oc_ref[...], wc_ref[...]))
    y = _dot(merged.astype(BF16), wo_ref[...])
    y = y * lax.rsqrt(jnp.mean(y * y, axis=-1, keepdims=True) + EPS) * g_ref[...]
    o_ref[...] = x_ref[...] + y


def _merge(oa, ob, oc, p2, x2d, wa, wb, wc, wo, g):
    m = x2d.shape[0]
    tm = min(m, 256)
    row = lambda w: pl.BlockSpec((tm, w), lambda i: (i, 0))
    gate = lambda blk: pl.BlockSpec((tm, D_MODEL), lambda i: (i, blk))
    const = lambda shape: pl.BlockSpec(shape, lambda i: (0, 0), pipeline_mode=pl.Buffered(1))
    return pl.pallas_call(
        _merge_kernel,
        grid=(m // tm,),
        in_specs=[row(WIDTH), row(WIDTH), row(WIDTH), gate(6), gate(7), gate(8),
                  const((WIDTH, D_MODEL)), const((WIDTH, D_MODEL)), const((WIDTH, D_MODEL)),
                  const((D_MODEL, D_MODEL)), row(D_MODEL), const((1, D_MODEL))],
        out_specs=row(D_MODEL),
        out_shape=jax.ShapeDtypeStruct((m, D_MODEL), F32),
        compiler_params=pltpu.CompilerParams(
            dimension_semantics=("arbitrary",), vmem_limit_bytes=V7X_VMEM_LIMIT),
        name="merge",
    )(oa, ob, oc, p2, p2, p2, wa, wb, wc, wo, x2d, g)


def _retention_constants():
    log_gamma = jnp.log1p(-jnp.exp2(-5.0 - jnp.arange(HEADS, dtype=F32)))
    idx = jnp.arange(CHUNK, dtype=F32)
    rel = idx[:, None] - idx[None, :]
    lg = log_gamma[:, None, None]
    dmat = jnp.where(rel >= 0, jnp.exp(lg * jnp.maximum(rel, 0.0)), 0.0)
    xi = jnp.exp(log_gamma[:, None] * (idx + 1.0))
    zeta = jnp.exp(log_gamma[:, None] * (CHUNK - 1.0 - idx))
    gblk = jnp.exp(log_gamma * CHUNK)
    bc = lambda v: jnp.broadcast_to(v[:, :, None], (HEADS, v.shape[1], HEAD_DIM))
    return dmat, bc(xi), bc(zeta), bc(gblk[:, None])


def _rotary_tables(pos):
    half = HEAD_DIM // 2
    inv = ROPE_BASE ** (-jnp.arange(half, dtype=F32) / half)
    ang = pos.astype(F32)[:, None] * inv[None, :]
    cos, sin = jnp.cos(ang), jnp.sin(ang)
    return jnp.concatenate([cos, cos], axis=1), jnp.concatenate([-sin, sin], axis=1)


def _rev_bias(rel_bias):
    n_off = BAND_KEYS + CHUNK - 1
    head = rel_bias[:, MAX_REL - (CHUNK - 1):].astype(F32)
    edge = jnp.broadcast_to(head[:, -1:], (rel_bias.shape[0], n_off - head.shape[1]))
    ext = jnp.concatenate([head, edge], axis=1)
    return jnp.pad(ext[:, ::-1], ((0, 0), (0, BIAS_LANES - n_off)))


def _lane_row(v, lane0):
    return jnp.zeros((1, GATE_LANES), F32).at[0, lane0:lane0 + v.shape[0]].set(v.astype(F32))


def _group_layer(x, pos, cache, conv_buf, s_delta, s_ret, lw, ret_consts):
    (norm_pre, norm_post, w_main, w_gate, bias, conv_w, al_row, dt_row, dn_row, rn_row,
     wa, wb, wc, wo) = lw
    b, t, _ = x.shape
    x2d = x.reshape(b * t, D_MODEL)
    p2, gt2 = _inproj(x2d, norm_pre, w_main, w_gate)
    p3 = p2.reshape(b, t, N_MAIN)
    gt3 = gt2.reshape(b, t, GATE_LANES)

    oa, new_k, new_v = _attn(p3, bias, cache)

    conv0 = None if conv_buf is None else jnp.pad(conv_buf, ((0, 0), (8 - (CONV_W - 1), 0), (0, 0)))
    ob, new_sd, conv_tail = _delta(p3, gt3, conv_w, al_row, dt_row, dn_row, conv0, s_delta)
    new_conv = conv_tail[:, 8 - (CONV_W - 1):, :]

    cc, ss = _rotary_tables(pos)
    oc, new_sr = _ret(p3, cc, ss, ret_consts, rn_row, s_ret)

    y2d = _merge(oa.reshape(b * t, WIDTH), ob.reshape(b * t, WIDTH), oc.reshape(b * t, WIDTH),
                 p2, x2d, wa, wb, wc, wo, norm_post)
    return y2d.reshape(b, t, D_MODEL), new_k, new_v, new_conv, new_sd, new_sr


def kernel(x_prompt, x_sample, cache_attn_k, cache_attn_v, state_conv, state_delta, state_ret,
           norm_pre, norm_post, w_in, attn_rel_bias, conv_w, delta_a_log, delta_dt_bias,
           delta_norm, ret_norm, w_branch_a, w_branch_b, w_branch_c, w_out):
    depth = w_in.shape[0]
    tp, ts = x_prompt.shape[1], x_sample.shape[1]
    pos_p = jnp.arange(tp, dtype=jnp.int32)
    pos_s = PAST_LEN + jnp.arange(ts, dtype=jnp.int32)
    ret_consts = _retention_constants()
    w_in_t = jnp.swapaxes(w_in, 1, 2)
    xp, xs = x_prompt, x_sample
    acc_p = [[] for _ in range(5)]
    acc_s = [[] for _ in range(5)]
    for l in range(depth):
        w_main, w_gate = _weight_prep(w_in_t, l)
        lw =(norm_pre[l][None, :], norm_post[l][None, :], w_main, w_gate,
              _rev_bias(attn_rel_bias[l]), conv_w[l],
              _lane_row(delta_a_log[l], HEADS), _lane_row(delta_dt_bias[l], HEADS),
              delta_norm[l][None, :].astype(F32), ret_norm[l].reshape(1, WIDTH).astype(F32),
              w_branch_a[l].astype(BF16), w_branch_b[l].astype(BF16), w_branch_c[l].astype(BF16),
              w_out[l].astype(BF16))
        xp, *st_p = _group_layer(xp, pos_p, None, None, None, None, lw, ret_consts)
        xs, *st_s = _group_layer(xs, pos_s, (cache_attn_k, cache_attn_v, l), state_conv[l],
                                 state_delta[l], state_ret[l], lw, ret_consts)
        for acc, val in zip(acc_p, st_p):
            acc.append(val)
        for acc, val in zip(acc_s, st_s):
            acc.append(val)
    outs_p = [jnp.stack(a, axis=0) for a in acc_p]
    outs_s = [jnp.stack(a, axis=0) for a in acc_s]
    return (xp, xs, *outs_p, *outs_s)
```

```python
import functools

import jax
import jax.numpy as jnp
import numpy as np
from jax import lax
from jax.experimental import pallas as pl
from jax.experimental.pallas import tpu as pltpu

F32 = jnp.float32
BF16 = jnp.bfloat16

D_MODEL = 2048
CHUNK = 64
HEAD_DIM = 128
HEADS = 8
WIDTH = HEADS * HEAD_DIM
BAND_CHUNKS = 8
BAND_ROWS = BAND_CHUNKS * CHUNK
BAND_KEYS = BAND_ROWS + CHUNK
BIAS_LANES = 640
MAX_REL = 128
CONV_W = 4
PAST_LEN = 2048
ROPE_BASE = 10000.0
EPS = 1e-6
NEG_INF = -1e30
SCALE = HEAD_DIM ** -0.5
LOG2E = 1.4426950408889634

N_MAIN = 18 * WIDTH
GATE_COL0 = 8 * WIDTH
GATE_LANES = 128
V7X_VMEM_LIMIT = 60 * 1024 * 1024
MIXER_CHUNKS_PER_STEP = 4
RET_CHUNKS_PER_STEP = 8
ATTN_GROUP_CHUNKS = 2

_NT = (((1,), (1,)), ((), ()))
_TN = (((0,), (0,)), ((), ()))


def _dot(a, b):
    return jnp.dot(a, b, preferred_element_type=F32)


def _dot_nt(a, b):
    return lax.dot_general(a, b, _NT, preferred_element_type=F32)


def _dot_tn(a, b):
    return lax.dot_general(a, b, _TN, preferred_element_type=F32)


def _silu(x):
    return x * jax.nn.sigmoid(x)


def _softplus(x):
    return jnp.maximum(x, 0.0) + jnp.log1p(jnp.exp(-jnp.abs(x)))


PREP_COLS = 512
N_GATE = 2 * HEADS
assert GATE_COL0 % PREP_COLS == 0 and N_GATE % 8 == 0 and N_GATE <= GATE_LANES


def _weight_prep_kernel(a_ref, b_ref, o_ref, og_ref):
    n = pl.program_id(0)
    first_shifted = GATE_COL0 // PREP_COLS

    @pl.when(n < first_shifted)
    def _():
        o_ref[...] = jnp.transpose(a_ref[...]).astype(BF16)

    @pl.when(n >= first_shifted)
    def _():
        src = jnp.concatenate([a_ref[N_GATE:PREP_COLS, :], b_ref[...]], axis=0)
        o_ref[...] = jnp.transpose(src).astype(BF16)

    @pl.when(n == first_shifted)
    def _():
        pad = jnp.zeros((GATE_LANES - N_GATE, D_MODEL), F32)
        og_ref[...] = jnp.concatenate([a_ref[0:N_GATE, :], pad], axis=0).astype(BF16)


def _weight_prep(w_in_t, layer):
    assert w_in_t.shape[1] == N_MAIN + N_GATE
    return pl.pallas_call(
        _weight_prep_kernel,
        grid=(N_MAIN // PREP_COLS,),
        in_specs=[pl.BlockSpec((None, PREP_COLS, D_MODEL), lambda n: (layer, n, 0)),
                  pl.BlockSpec((None, N_GATE, D_MODEL),
                               lambda n: (layer, (n + 1) * (PREP_COLS // N_GATE), 0))],
        out_specs=[pl.BlockSpec((D_MODEL, PREP_COLS), lambda n: (0, n)),
                   pl.BlockSpec((GATE_LANES, D_MODEL), lambda n: (0, 0))],
        out_shape=[jax.ShapeDtypeStruct((D_MODEL, N_MAIN), BF16),
                   jax.ShapeDtypeStruct((GATE_LANES, D_MODEL), BF16)],
        compiler_params=pltpu.CompilerParams(
            dimension_semantics=("arbitrary",), vmem_limit_bytes=V7X_VMEM_LIMIT),
        name="wprep",
    )(w_in_t, w_in_t)


def _inproj_kernel(x_ref, g_ref, w_ref, wg_ref, o_ref, og_ref, h_ref):
    @pl.when(pl.program_id(1) == 0)
    def _():
        x = x_ref[...]
        ms = jnp.mean(x * x, axis=-1, keepdims=True)
        h = (x * lax.rsqrt(ms + EPS) * g_ref[...]).astype(BF16)
        h_ref[...] = h
        og_ref[...] = _dot_nt(h, wg_ref[...])

    o_ref[...] = _dot(h_ref[...], w_ref[...])


def _inproj(x2d, g, w_main, w_gate):
    m = x2d.shape[0]
    tm = min(m, 1024)
    tn = 2048
    return pl.pallas_call(
        _inproj_kernel,
        grid=(m // tm, N_MAIN // tn),
        in_specs=[
            pl.BlockSpec((tm, D_MODEL), lambda i, n: (i, 0)),
            pl.BlockSpec((1, D_MODEL), lambda i, n: (0, 0)),
            pl.BlockSpec((D_MODEL, tn), lambda i, n: (0, n)),
            pl.BlockSpec((GATE_LANES, D_MODEL), lambda i, n: (0, 0)),
        ],
        out_specs=[
            pl.BlockSpec((tm, tn), lambda i, n: (i, n)),
            pl.BlockSpec((tm, GATE_LANES), lambda i, n: (i, 0)),
        ],
        out_shape=[
            jax.ShapeDtypeStruct((m, N_MAIN), F32),
            jax.ShapeDtypeStruct((m, GATE_LANES), F32),
        ],
        scratch_shapes=[pltpu.VMEM((tm, D_MODEL), BF16)],
        compiler_params=pltpu.CompilerParams(
            dimension_semantics=("arbitrary", "arbitrary"),
            vmem_limit_bytes=V7X_VMEM_LIMIT),
        name="inproj",
    )(x2d, g, w_main, w_gate)


def _attn_kernel(*refs, nq, gq, has_cache):
    if has_cache:
        (q_ref, ko_ref, vo_ref, z_ref, rb_ref, kc_ref, vc_ref,
         o_ref, nk_ref, nv_ref, kt, vcat, bias_scr) = refs
    else:
        (q_ref, ko_ref, vo_ref, z_ref, rb_ref,
         o_ref, nk_ref, nv_ref, kt, vcat, bias_scr) = refs
    t = pl.program_id(1)
    nt = pl.num_programs(1)
    rows = nq * CHUNK
    grows = gq * CHUNK
    gw = BAND_ROWS + grows

    @pl.when(jnp.logical_and(pl.program_id(0) == 0, t == 0))
    def _():
        lane = lax.broadcasted_iota(jnp.int32, (CHUNK, BIAS_LANES), 1)
        for h in range(HEADS):
            wide = jnp.broadcast_to(rb_ref[h:h + 1, :], (CHUNK, BIAS_LANES))
            for e in range(gq):
                shift = (BIAS_LANES - (CHUNK - 1) + e * CHUNK) % BIAS_LANES
                tab = pltpu.roll(wide, shift, 1, stride=1, stride_axis=0) * LOG2E
                tab = jnp.where(lane < e * CHUNK, NEG_INF, tab)
                tab = jnp.where(lane >= e * CHUNK + BAND_KEYS, NEG_INF, tab)
                bias_scr[h, e * CHUNK:(e + 1) * CHUNK, :] = tab

    eye16 = (lax.broadcasted_iota(jnp.int32, (HEAD_DIM, HEAD_DIM), 0)
             == lax.broadcasted_iota(jnp.int32, (HEAD_DIM, HEAD_DIM), 1)).astype(BF16)

    def transpose16(x16):
        return _dot_nt(eye16, x16).astype(BF16)

    if has_cache:
        for h in range(HEADS):
            sl = slice(h * HEAD_DIM, (h + 1) * HEAD_DIM)
            kt[h, :, 0:BAND_ROWS] = transpose16(kc_ref[:, h, :].astype(BF16))
            vcat[0:BAND_ROWS, sl] = vc_ref[:, h, :].astype(BF16)
    else:
        @pl.when(t == 0)
        def _():
            kt[:, :, 0:BAND_ROWS] = jnp.zeros((HEADS, HEAD_DIM, BAND_ROWS), BF16)
            vcat[0:BAND_ROWS, :] = jnp.zeros((BAND_ROWS, WIDTH), BF16)

        @pl.when(t > 0)
        def _():
            kt[:, :, 0:BAND_ROWS] = kt[:, :, rows:rows + BAND_ROWS]
            vcat[0:BAND_ROWS, :] = vcat[rows:rows + BAND_ROWS, :]
    for h in range(HEADS):
        sl = slice(h * HEAD_DIM, (h + 1) * HEAD_DIM)
        kt[h, :, BAND_ROWS:BAND_ROWS + rows] = transpose16(ko_ref[:, sl].astype(BF16))
    vcat[BAND_ROWS:BAND_ROWS + rows, :] = vo_ref[...].astype(BF16)

    col = lax.broadcasted_iota(jnp.int32, (grows, gw), 1)
    for h in range(HEADS):
        sl = slice(h * HEAD_DIM, (h + 1) * HEAD_DIM)
        bias = bias_scr[h][:, :gw]
        scores = []
        for g in range(nq // gq):
            r0 = g * grows
            q = (q_ref[r0:r0 + grows, sl] * (SCALE * LOG2E)).astype(BF16)
            scores.append(_dot(q, kt[h, :, r0:r0 + gw]))
        probs = []
        for g in range(nq // gq):
            s = scores[g] + bias
            if not has_cache:
                first_valid = jnp.where(t == 0, BAND_ROWS - g * grows, 0)
                s = jnp.where(col < first_valid, NEG_INF, s)
            m = jnp.max(s, axis=-1, keepdims=True)
            p = jnp.exp2(s - m)
            probs.append((p.astype(BF16), jnp.sum(p, axis=-1, keepdims=True)))
        outs = []
        for g in range(nq // gq):
            r0 = g * grows
            outs.append(_dot(probs[g][0], vcat[r0:r0 + gw, sl]))
        for g in range(nq // gq):
            r0 = g * grows
            o = outs[g] / probs[g][1]
            o_ref[r0:r0 + grows, sl] = (o * _silu(z_ref[r0:r0 + grows, sl])).astype(BF16)

    @pl.when(t == nt - 1)
    def _():
        if rows < BAND_ROWS:
            nk_ref[0:BAND_ROWS - rows] = kc_ref[rows:BAND_ROWS]
            nv_ref[0:BAND_ROWS - rows] = vc_ref[rows:BAND_ROWS]
        for h in range(HEADS):
            sl = slice(h * HEAD_DIM, (h + 1) * HEAD_DIM)
            nk_ref[BAND_ROWS - rows:BAND_ROWS, h, :] = ko_ref[:, sl]
            nv_ref[BAND_ROWS - rows:BAND_ROWS, h, :] = vo_ref[:, sl]


def _attn(p3, rev_bias, cache):
    b, t, _ = p3.shape
    has_cache = cache is not None
    rows = CHUNK if has_cache else BAND_ROWS
    assert t % rows == 0 and (not has_cache or t == CHUNK)
    nq = rows // CHUNK
    gq = min(nq, ATTN_GROUP_CHUNKS)
    assert BAND_ROWS + gq * CHUNK <= BIAS_LANES
    cur = lambda blk: pl.BlockSpec((None, rows, WIDTH), lambda i, tt: (i, tt, blk))
    newest =pl.BlockSpec((None, BAND_ROWS, HEADS, HEAD_DIM), lambda i, tt: (i, 0, 0, 0))
    in_specs = [cur(0), cur(1), cur(2), cur(3),
                pl.BlockSpec((HEADS, BIAS_LANES), lambda i, tt: (0, 0))]
    args = [p3, p3, p3, p3, rev_bias]
    if has_cache:
        ck, cv, layer = cache
        assert ck.shape[2:] == (BAND_ROWS, HEADS, HEAD_DIM)
        cached = pl.BlockSpec((None, None, BAND_ROWS, HEADS, HEAD_DIM),
                              lambda i, tt: (layer, i, 0, 0, 0))
        in_specs += [cached, cached]
        args += [ck, cv]
    return pl.pallas_call(
        functools.partial(_attn_kernel, nq=nq, gq=gq, has_cache=has_cache),
        grid=(b, t // rows),
        in_specs=in_specs,
        out_specs=[cur(0), newest, newest],
        out_shape=[jax.ShapeDtypeStruct((b, t, WIDTH), BF16),
                   jax.ShapeDtypeStruct((b, BAND_ROWS, HEADS, HEAD_DIM), F32),
                   jax.ShapeDtypeStruct((b, BAND_ROWS, HEADS, HEAD_DIM), F32)],
        scratch_shapes=[pltpu.VMEM((HEADS, HEAD_DIM, BAND_ROWS + rows), BF16),
                        pltpu.VMEM((BAND_ROWS + rows, WIDTH), BF16),
                        pltpu.VMEM((HEADS, gq * CHUNK, BIAS_LANES), F32)],
        compiler_params=pltpu.CompilerParams(
            dimension_semantics=("arbitrary", "arbitrary"), vmem_limit_bytes=V7X_VMEM_LIMIT),
        name="attn",
    )(*args)


def _delta_kernel(*refs, has_state, nch):
    if has_state:
        (q_ref, k_ref, v_ref, z_ref, gt_ref, cw_ref, al_ref, dt_ref, dn_ref, c0_ref, s0_ref,
         o_ref, sout_ref, tail_ref, xbuf, act, st) = refs
    else:
        (q_ref, k_ref, v_ref, z_ref, gt_ref, cw_ref, al_ref, dt_ref, dn_ref,
         o_ref, sout_ref, tail_ref, xbuf, act, st) = refs
    c = pl.program_id(1)
    nc = pl.num_programs(1)
    rows = nch * CHUNK

    lane_tiles = [(p, j, slice(p * WIDTH + j * HEAD_DIM, p * WIDTH + (j + 1) * HEAD_DIM))
                  for p in range(3) for j in range(HEADS)]

    @pl.when(c == 0)
    def _():
        if has_state:
            for p, j, cols in lane_tiles:
                xbuf[p, j, 0:8, :] = c0_ref[:, cols]
            st[...] = s0_ref[...]
        else:
            xbuf[:, :, 0:8, :] = jnp.zeros((3, HEADS, 8, HEAD_DIM), F32)
            st[...] = jnp.zeros_like(st)

    @pl.when(c > 0)
    def _():
        xbuf[:, :, 0:8, :] = xbuf[:, :, rows:rows + 8, :]

    for p, src in enumerate((q_ref, k_ref, v_ref)):
        for j in range(HEADS):
            xbuf[p, j, 8:8 + rows, :] = src[:, j * HEAD_DIM:(j + 1) * HEAD_DIM]

    for p, j, cols in lane_tiles:
        y = None
        for i in range(CONV_W):
            start = 8 - (CONV_W - 1) + i
            term = xbuf[p, j, start:start + rows, :] * cw_ref[i:i + 1, cols]
            y = term if y is None else y + term
        act[p, :, j * HEAD_DIM:(j + 1) * HEAD_DIM] = _silu(y)

    ri = lax.broadcasted_iota(jnp.int32, (CHUNK, CHUNK), 0)
    ci = lax.broadcasted_iota(jnp.int32, (CHUNK, CHUNK), 1)
    causal = ri >= ci
    strict = ri > ci
    ltri = causal.astype(F32)
    dn = dn_ref[...]

    chains = []
    for cc in range(nch):
        rs = slice(cc * CHUNK, (cc + 1) * CHUNK)
        gt = gt_ref[rs, :]
        beta_all = jax.nn.sigmoid(gt)
        g_all = -jnp.exp(al_ref[...]) * _softplus(gt + dt_ref[...])
        gcum = jnp.dot(ltri, g_all, precision=lax.Precision.HIGHEST,
                       preferred_element_type=F32)
        gcum_t = jnp.transpose(gcum)
        eg_all = jnp.exp(gcum)
        glast = gcum[CHUNK - 1:CHUNK, :]
        tail_all = jnp.exp(glast - gcum)
        gblk_all = jnp.exp(glast)
        for h in range(HEADS):
            sl = slice(h * HEAD_DIM, (h + 1) * HEAD_DIM)
            qh, kh, vh = act[0, rs, sl], act[1, rs, sl], act[2, rs, sl]
            qh = qh * lax.rsqrt(jnp.sum(qh * qh, axis=-1, keepdims=True) + EPS)
            kh = kh * lax.rsqrt(jnp.sum(kh * kh, axis=-1, keepdims=True) + EPS)
            beta = beta_all[:, h:h + 1]
            gl = HEADS + h
            eg = eg_all[:, gl:gl + 1]
            diff = gcum[:, gl:gl + 1] - gcum_t[gl:gl + 1, :]
            kbeta = kh * beta
            qs = qh * SCALE
            chains.append(dict(
                rs=rs, sl=sl, h=h,
                decay=jnp.exp(jnp.where(causal, diff, -jnp.inf)),
                k16=kh.astype(BF16), kbeta16=kbeta.astype(BF16), qs16=qs.astype(BF16),
                qdec=qs * eg,
                ktail16=(kh * tail_all[:, gl:gl + 1]).astype(BF16),
                gblk=gblk_all[:, gl:gl + 1],
                x=jnp.concatenate([vh * beta, kbeta * eg], axis=1)))

    for ch in chains:
        ch["a"] = jnp.where(strict, _dot_nt(ch["kbeta16"], ch["k16"]) * ch["decay"], 0.0)
    for ch in chains:
        ch["intra16"] = (_dot_nt(ch["qs16"], ch["k16"]) * ch["decay"]).astype(BF16)

    def same_block(size):
        shift = size.bit_length() - 1
        return jnp.right_shift(ri, shift) == jnp.right_shift(ci, shift)

    eye = (ri == ci).astype(F32)
    for ch in chains:
        n1 = jnp.where(same_block(8), -ch["a"], 0.0)
        ch["n1_16"] = n1.astype(BF16)
        ch["p"] = eye + n1
    for ch in chains:
        ch["n2"] = _dot(ch["n1_16"], ch["n1_16"])
    for ch in chains:
        ch["n2_16"] = ch["n2"].astype(BF16)
        ch["n4"] = _dot(ch["n2_16"], ch["n2_16"])
        ch["p"] = ch["p"] + _dot(ch["p"].astype(BF16), ch["n2_16"])
    for ch in chains:
        ch["t"] = ch["p"] + _dot(ch["p"].astype(BF16), ch["n4"].astype(BF16))
    for size in (8, 16, 32):
        couple = jnp.logical_and(same_block(2 * size), jnp.logical_not(same_block(size)))
        for ch in chains:
            ch["t16"] = ch["t"].astype(BF16)
            ch["te"] = _dot(ch["t16"], jnp.where(couple, ch["a"], 0.0).astype(BF16))
        for ch in chains:
            ch["t"] = ch["t"] - _dot(ch["te"].astype(BF16), ch["t16"])
    for ch in chains:
        ch["x"] = _dot(ch["t"].astype(BF16), ch["x"].astype(BF16))

    for ch in chains:
        x16 = ch["x"].astype(BF16)
        ix = _dot(ch["intra16"], x16)
        kx = _dot_tn(ch["ktail16"], x16)
        ch["o_intra"] = ix[:, :HEAD_DIM]
        ch["qeff16"] = (ch["qdec"] - ix[:, HEAD_DIM:]).astype(BF16)
        ch["b"] = kx[:, :HEAD_DIM]
        ch["p16"] = kx[:, HEAD_DIM:].astype(BF16)

    state = [st[h] for h in range(HEADS)]
    for cc in range(nch):
        for ch in chains[cc * HEADS:(cc + 1) * HEADS]:
            h, rs, sl = ch["h"], ch["rs"], ch["sl"]
            s16 = state[h].astype(BF16)
            o = _dot(ch["qeff16"], s16) + ch["o_intra"]
            state[h] = state[h] * ch["gblk"] - _dot(ch["p16"], s16) + ch["b"]
            o = o * lax.rsqrt(jnp.mean(o * o, axis=-1, keepdims=True) + EPS) * dn
            o_ref[rs, sl] = (o * _silu(z_ref[rs, sl])).astype(BF16)
    for h in range(HEADS):
        st[h] = state[h]

    @pl.when(c == nc - 1)
    def _():
        sout_ref[...] = st[...]
        for p, j, cols in lane_tiles:
            tail_ref[:, cols] = xbuf[p, j, rows:rows + 8, :]


def _delta(p3, gt3, conv_w, al_row, dt_row, dn_row, conv0, s0):
    b, t, _ = p3.shape
    has_state = s0 is not None
    nch = min(MIXER_CHUNKS_PER_STEP, t // CHUNK)
    rows = nch * CHUNK
    assert t % rows == 0
    col = lambda blk: pl.BlockSpec((None, rows, WIDTH), lambda i, c: (i, c, blk))
    full = lambda shape: pl.BlockSpec(shape, lambda i, c: (0,) * len(shape))
    in_specs = [col(4), col(5), col(6), col(7),
                pl.BlockSpec((None, rows, GATE_LANES), lambda i, c: (i, c, 0)),
                full((CONV_W, 3 * WIDTH)), full((1, GATE_LANES)), full((1, GATE_LANES)),
                full((1, HEAD_DIM))]
    args = [p3, p3, p3, p3, gt3, conv_w, al_row, dt_row, dn_row]
    if has_state:
        in_specs += [pl.BlockSpec((None, 8, 3 * WIDTH), lambda i, c: (i, 0, 0)),
                     pl.BlockSpec((None, HEADS, HEAD_DIM, HEAD_DIM), lambda i, c: (i, 0, 0, 0))]
        args += [conv0, s0]
    return pl.pallas_call(
        functools.partial(_delta_kernel, has_state=has_state, nch=nch),
        grid=(b, t // rows),
        in_specs=in_specs,
        out_specs=[pl.BlockSpec((None, rows, WIDTH), lambda i, c: (i, c, 0)),
                   pl.BlockSpec((None, HEADS, HEAD_DIM, HEAD_DIM), lambda i, c: (i, 0, 0, 0)),
                   pl.BlockSpec((None, 8, 3 * WIDTH), lambda i, c: (i, 0, 0))],
        out_shape=[jax.ShapeDtypeStruct((b, t, WIDTH), BF16),
                   jax.ShapeDtypeStruct((b, HEADS, HEAD_DIM, HEAD_DIM), F32),
                   jax.ShapeDtypeStruct((b, 8, 3 * WIDTH), F32)],
        scratch_shapes=[pltpu.VMEM((3, HEADS, rows + 8, HEAD_DIM), F32),
                        pltpu.VMEM((3, rows, WIDTH), F32),
                        pltpu.VMEM((HEADS, HEAD_DIM, HEAD_DIM), F32)],
        compiler_params=pltpu.CompilerParams(
            dimension_semantics=("arbitrary", "arbitrary"), vmem_limit_bytes=V7X_VMEM_LIMIT),
        name="delta",
    )(*args)


def _ret_kernel(*refs, has_state, nch):
    if has_state:
        (q_ref, k_ref, v_ref, z_ref, cc_ref, ss_ref, dm_ref, xi_ref, zeta_ref, gb_ref, rn_ref,
         s0_ref, o_ref, sout_ref, st) = refs
    else:
        (q_ref, k_ref, v_ref, z_ref, cc_ref, ss_ref, dm_ref, xi_ref, zeta_ref, gb_ref, rn_ref,
         o_ref, sout_ref, st) = refs
    c = pl.program_id(1)
    nc = pl.num_programs(1)

    @pl.when(c == 0)
    def _():
        if has_state:
            st[...] = s0_ref[...]
        else:
            st[...] = jnp.zeros_like(st)

    chains = []
    for cc in range(nch):
        rs = slice(cc * CHUNK, (cc + 1) * CHUNK)
        cos2 = cc_ref[rs, :]
        sin2 = ss_ref[rs, :]
        for h in range(HEADS):
            sl = slice(h * HEAD_DIM, (h + 1) * HEAD_DIM)
            q = q_ref[rs, sl]
            k = k_ref[rs, sl]
            qh = q * cos2 + pltpu.roll(q, HEAD_DIM // 2, axis=1) * sin2
            kh = (k * cos2 + pltpu.roll(k, HEAD_DIM // 2, axis=1) * sin2) * SCALE
            chains.append(dict(rs=rs, sl=sl, h=h, q16=qh.astype(BF16), k16=kh.astype(BF16),
                               kz16=(kh * zeta_ref[h]).astype(BF16),
                               v16=v_ref[rs, sl].astype(BF16)))
    for ch in chains:
        ch["intra16"] = (_dot_nt(ch["q16"], ch["k16"]) * dm_ref[ch["h"]]).astype(BF16)
    for ch in chains:
        ch["kv"] = _dot_tn(ch["kz16"], ch["v16"])
    for ch in chains:
        ch["o_intra"] = _dot(ch["intra16"], ch["v16"])

    state = [st[h] for h in range(HEADS)]
    for cc in range(nch):
        for ch in chains[cc * HEADS:(cc + 1) * HEADS]:
            h, rs, sl = ch["h"], ch["rs"], ch["sl"]
            ch["o"] = ch["o_intra"] + _dot(ch["q16"], state[h].astype(BF16)) * xi_ref[h]
            state[h] = state[h] * gb_ref[h] + ch["kv"]
    for h in range(HEADS):
        st[h] = state[h]

    for ch in chains:
        ch["d"] = ch["o"] - jnp.mean(ch["o"], axis=-1, keepdims=True)
    for ch in chains:
        ch["var"] = jnp.mean(ch["d"] * ch["d"], axis=-1, keepdims=True)
    for ch in chains:
        rs, sl = ch["rs"], ch["sl"]
        o = ch["d"] * lax.rsqrt(ch["var"] + EPS) * rn_ref[:, sl]
        o_ref[rs, sl] = (o * _silu(z_ref[rs, sl])).astype(BF16)

    @pl.when(c == nc - 1)
    def _():
        sout_ref[...] = st[...]


def _ret(p3, cc, ss, consts, rn_row, s0):
    b, t, _ = p3.shape
    has_state = s0 is not None
    nch = min(RET_CHUNKS_PER_STEP, t // CHUNK)
    rows = nch * CHUNK
    assert t % rows == 0
    dmat, xi, zeta, gblk = consts
    col = lambda blk: pl.BlockSpec((None, rows, WIDTH), lambda i, c: (i, c, blk))
    full = lambda shape: pl.BlockSpec(shape, lambda i, c: (0,) * len(shape))
    tab = pl.BlockSpec((rows, HEAD_DIM), lambda i, c: (c, 0))
    in_specs = [col(8), col(9), col(10), col(11), tab, tab,
                full((HEADS, CHUNK, CHUNK)), full((HEADS, CHUNK, HEAD_DIM)),
                full((HEADS, CHUNK, HEAD_DIM)), full((HEADS, 1, HEAD_DIM)), full((1, WIDTH))]
    args = [p3, p3, p3, p3, cc, ss, dmat, xi, zeta, gblk, rn_row]
    if has_state:
        in_specs.append(pl.BlockSpec((None, HEADS, HEAD_DIM, HEAD_DIM), lambda i, c: (i, 0, 0, 0)))
        args.append(s0)
    return pl.pallas_call(
        functools.partial(_ret_kernel, has_state=has_state, nch=nch),
        grid=(b, t // rows),
        in_specs=in_specs,
        out_specs=[pl.BlockSpec((None, rows, WIDTH), lambda i, c: (i, c, 0)),
                   pl.BlockSpec((None, HEADS, HEAD_DIM, HEAD_DIM), lambda i, c: (i, 0, 0, 0))],
        out_shape=[jax.ShapeDtypeStruct((b, t, WIDTH), BF16),
                   jax.ShapeDtypeStruct((b, HEADS, HEAD_DIM, HEAD_DIM), F32)],
        scratch_shapes=[pltpu.VMEM((HEADS, HEAD_DIM, HEAD_DIM), F32)],
        compiler_params=pltpu.CompilerParams(
            dimension_semantics=("arbitrary", "arbitrary"), vmem_limit_bytes=V7X_VMEM_LIMIT),
        name="ret",
    )(*args)


def _merge_kernel(oa_ref, ob_ref, oc_ref, ga_ref, gb_ref, gc_ref, wa_ref, wb_ref, wc_ref,
                  wo_ref, x_ref, g_ref, o_ref):
    merged = (jax.nn.sigmoid(ga_ref[...]) * _dot(oa_ref[...], wa_ref[...])
              + jax.nn.sigmoid(gb_ref[...]) * _dot(ob_ref[...], wb_ref[...])
              + jax.nn.sigmoid(gc_ref[...]) * _dot(oc_ref[...], wc_ref[...]))
    y = _dot(merged.astype(BF16), wo_ref[...])
    y = y * lax.rsqrt(jnp.mean(y * y, axis=-1, keepdims=True) + EPS) * g_ref[...]
    o_ref[...] = x_ref[...] + y


def _merge(oa, ob, oc, p2, x2d, wa, wb, wc, wo, g):
    m = x2d.shape[0]
    tm = min(m, 256)
    row = lambda w: pl.BlockSpec((tm, w), lambda i: (i, 0))
    gate = lambda blk: pl.BlockSpec((tm, D_MODEL), lambda i: (i, blk))
    const = lambda shape: pl.BlockSpec(shape, lambda i: (0, 0), pipeline_mode=pl.Buffered(1))
    return pl.pallas_call(
        _merge_kernel,
        grid=(m // tm,),
        in_specs=[row(WIDTH), row(WIDTH), row(WIDTH), gate(6), gate(7), gate(8),
                  const((WIDTH, D_MODEL)), const((WIDTH, D_MODEL)), const((WIDTH, D_MODEL)),
                  const((D_MODEL, D_MODEL)), row(D_MODEL), const((1, D_MODEL))],
        out_specs=row(D_MODEL),
        out_shape=jax.ShapeDtypeStruct((m, D_MODEL), F32),
        compiler_params=pltpu.CompilerParams(
            dimension_semantics=("arbitrary",), vmem_limit_bytes=V7X_VMEM_LIMIT),
        name="merge",
    )(oa, ob, oc, p2, p2, p2, wa, wb, wc, wo, x2d, g)


def _retention_constants():
    log_gamma = jnp.log1p(-jnp.exp2(-5.0 - jnp.arange(HEADS, dtype=F32)))
    idx = jnp.arange(CHUNK, dtype=F32)
    rel = idx[:, None] - idx[None, :]
    lg = log_gamma[:, None, None]
    dmat = jnp.where(rel >= 0, jnp.exp(lg * jnp.maximum(rel, 0.0)), 0.0)
    xi = jnp.exp(log_gamma[:, None] * (idx + 1.0))
    zeta = jnp.exp(log_gamma[:, None] * (CHUNK - 1.0 - idx))
    gblk = jnp.exp(log_gamma * CHUNK)
    bc = lambda v: jnp.broadcast_to(v[:, :, None], (HEADS, v.shape[1], HEAD_DIM))
    return dmat, bc(xi), bc(zeta), bc(gblk[:, None])


def _rotary_tables(pos):
    half = HEAD_DIM // 2
    inv = ROPE_BASE ** (-jnp.arange(half, dtype=F32) / half)
    ang = pos.astype(F32)[:, None] * inv[None, :]
    cos, sin = jnp.cos(ang), jnp.sin(ang)
    return jnp.concatenate([cos, cos], axis=1), jnp.concatenate([-sin, sin], axis=1)


def _rev_bias(rel_bias):
    n_off = BAND_KEYS + CHUNK - 1
    head = rel_bias[:, MAX_REL - (CHUNK - 1):].astype(F32)
    edge = jnp.broadcast_to(head[:, -1:], (rel_bias.shape[0], n_off - head.shape[1]))
    ext = jnp.concatenate([head, edge], axis=1)
    return jnp.pad(ext[:, ::-1], ((0, 0), (0, BIAS_LANES - n_off)))


def _lane_row(v, lane0):
    return jnp.zeros((1, GATE_LANES), F32).at[0, lane0:lane0 + v.shape[0]].set(v.astype(F32))


def _group_layer(x, pos, cache, conv_buf, s_delta, s_ret, lw, ret_consts):
    (norm_pre, norm_post, w_main, w_gate, bias, conv_w, al_row, dt_row, dn_row, rn_row,
     wa, wb, wc, wo) = lw
    b, t, _ = x.shape
    x2d = x.reshape(b * t, D_MODEL)
    p2, gt2 = _inproj(x2d, norm_pre, w_main, w_gate)
    p3 = p2.reshape(b, t, N_MAIN)
    gt3 = gt2.reshape(b, t, GATE_LANES)

    oa, new_k, new_v = _attn(p3, bias, cache)

    conv0 = None if conv_buf is None else jnp.pad(conv_buf, ((0, 0), (8 - (CONV_W - 1), 0), (0, 0)))
    ob, new_sd, conv_tail = _delta(p3, gt3, conv_w, al_row, dt_row, dn_row, conv0, s_delta)
    new_conv = conv_tail[:, 8 - (CONV_W - 1):, :]

    cc, ss = _rotary_tables(pos)
    oc, new_sr = _ret(p3, cc, ss, ret_consts, rn_row, s_ret)

    y2d = _merge(oa.reshape(b * t, WIDTH), ob.reshape(b * t, WIDTH), oc.reshape(b * t, WIDTH),
                 p2, x2d, wa, wb, wc, wo, norm_post)
    return y2d.reshape(b, t, D_MODEL), new_k, new_v, new_conv, new_sd, new_sr


def kernel(x_prompt, x_sample, cache_attn_k, cache_attn_v, state_conv, state_delta, state_ret,
           norm_pre, norm_post, w_in, attn_rel_bias, conv_w, delta_a_log, delta_dt_bias,
           delta_norm, ret_norm, w_branch_a, w_branch_b, w_branch_c, w_out):
    depth = w_in.shape[0]
    tp, ts = x_prompt.shape[1], x_sample.shape[1]
    pos_p = jnp.arange(tp, dtype=jnp.int32)
    pos_s = PAST_LEN + jnp.arange(ts, dtype=jnp.int32)
    ret_consts = _retention_constants()
    w_in_t = jnp.swapaxes(w_in, 1, 2)
    xp, xs = x_prompt, x_sample
    acc_p = [[] for _ in range(5)]
    acc_s = [[] for _ in range(5)]
    for l in range(depth):
        w_main, w_gate = _weight_prep(w_in_t, l)
        lw =(norm_pre[l][None, :], norm_post[l][None, :], w_main, w_gate,
              _rev_bias(attn_rel_bias[l]), conv_w[l],
              _lane_row(delta_a_log[l], HEADS), _lane_row(delta_dt_bias[l], HEADS),
              delta_norm[l][None, :].astype(F32), ret_norm[l].reshape(1, WIDTH).astype(F32),
              w_branch_a[l].astype(BF16), w_branch_b[l].astype(BF16), w_branch_c[l].astype(BF16),
              w_out[l].astype(BF16))
        xp, *st_p = _group_layer(xp, pos_p, None, None, None, None, lw, ret_consts)
        xs, *st_s = _group_layer(xs, pos_s, (cache_attn_k, cache_attn_v, l), state_conv[l],
                                 state_delta[l], state_ret[l], lw, ret_consts)
        for acc, val in zip(acc_p, st_p):
            acc.append(val)
        for acc, val in zip(acc_s, st_s):
            acc.append(val)
    outs_p = [jnp.stack(a, axis=0) for a in acc_p]
    outs_s = [jnp.stack(a, axis=0) for a in acc_s]
    return (xp, xs, *outs_p, *outs_s)
```

```python
import functools

import jax
import jax.numpy as jnp
from jax import lax
from jax.experimental import pallas as pl
from jax.experimental.pallas import tpu as pltpu

F32 = jnp.float32
BF16 = jnp.bfloat16

D_MODEL = 2048
CHUNK = 64
HEAD_DIM = 128
HEADS = 8
WIDTH = HEADS * HEAD_DIM
BAND_CHUNKS = 8
BAND_ROWS = BAND_CHUNKS * CHUNK
BAND_KEYS = BAND_ROWS + CHUNK
BIAS_LANES = 640
MAX_REL = 128
CONV_W = 4
PAST_LEN = 2048
ROPE_BASE = 10000.0
EPS = 1e-6
NEG_INF = -1e30
SCALE = HEAD_DIM ** -0.5
LOG2E = 1.4426950408889634

N_MAIN = 18 * WIDTH
GATE_COL0 = 8 * WIDTH
GATE_LANES = 128
V7X_VMEM_LIMIT = 60 * 1024 * 1024
MIXER_CHUNKS_PER_STEP = 4
RET_CHUNKS_PER_STEP = 8
ATTN_GROUP_CHUNKS = 2
INPROJ_ROWS = 1024
INPROJ_COLS = 2048
MERGE_ROWS = 256
assert MAX_REL >= CHUNK - 1

_NT = (((1,), (1,)), ((), ()))
_TN = (((0,), (0,)), ((), ()))


def _dot(a, b):
    return jnp.dot(a, b, preferred_element_type=F32)


def _dot_nt(a, b):
    return lax.dot_general(a, b, _NT, preferred_element_type=F32)


def _dot_tn(a, b):
    return lax.dot_general(a, b, _TN, preferred_element_type=F32)


def _silu(x):
    return x * jax.nn.sigmoid(x)


def _softplus(x):
    return jnp.maximum(x, 0.0) + jnp.log1p(jnp.exp(-jnp.abs(x)))


PREP_COLS = 512
N_GATE = 2 * HEADS
assert GATE_COL0 % PREP_COLS == 0 and N_GATE % 8 == 0 and N_GATE <= GATE_LANES


def _weight_prep_kernel(a_ref, b_ref, o_ref, og_ref):
    n = pl.program_id(0)
    first_shifted = GATE_COL0 // PREP_COLS

    @pl.when(n < first_shifted)
    def _():
        o_ref[...] = jnp.transpose(a_ref[...]).astype(BF16)

    @pl.when(n >= first_shifted)
    def _():
        src = jnp.concatenate([a_ref[N_GATE:PREP_COLS, :], b_ref[...]], axis=0)
        o_ref[...] = jnp.transpose(src).astype(BF16)

    @pl.when(n == first_shifted)
    def _():
        pad = jnp.zeros((GATE_LANES - N_GATE, D_MODEL), F32)
        og_ref[...] = jnp.concatenate([a_ref[0:N_GATE, :], pad], axis=0).astype(BF16)


def _weight_prep(w_in_t, layer):
    assert w_in_t.shape[1] == N_MAIN + N_GATE
    return pl.pallas_call(
        _weight_prep_kernel,
        grid=(N_MAIN // PREP_COLS,),
        in_specs=[pl.BlockSpec((None, PREP_COLS, D_MODEL), lambda n: (layer, n, 0)),
                  pl.BlockSpec((None, N_GATE, D_MODEL),
                               lambda n: (layer, (n + 1) * (PREP_COLS // N_GATE), 0))],
        out_specs=[pl.BlockSpec((D_MODEL, PREP_COLS), lambda n: (0, n)),
                   pl.BlockSpec((GATE_LANES, D_MODEL), lambda n: (0, 0))],
        out_shape=[jax.ShapeDtypeStruct((D_MODEL, N_MAIN), BF16),
                   jax.ShapeDtypeStruct((GATE_LANES, D_MODEL), BF16)],
        compiler_params=pltpu.CompilerParams(
            dimension_semantics=("arbitrary",), vmem_limit_bytes=V7X_VMEM_LIMIT),
        name="wprep",
    )(w_in_t, w_in_t)


def _inproj_kernel(x_ref, g_ref, w_ref, wg_ref, o_ref, og_ref, h_ref):
    @pl.when(pl.program_id(1) == 0)
    def _():
        x = x_ref[...]
        ms = jnp.mean(x * x, axis=-1, keepdims=True)
        h = (x * lax.rsqrt(ms + EPS) * g_ref[...]).astype(BF16)
        h_ref[...] = h
        og_ref[...] = _dot_nt(h, wg_ref[...])

    o_ref[...] = _dot(h_ref[...], w_ref[...])


def _inproj(x2d, g, w_main, w_gate):
    m = x2d.shape[0]
    tm = min(m, INPROJ_ROWS)
    tn = INPROJ_COLS
    assert m % tm == 0 and N_MAIN % tn == 0
    return pl.pallas_call(
        _inproj_kernel,
        grid=(m // tm, N_MAIN // tn),
        in_specs=[
            pl.BlockSpec((tm, D_MODEL), lambda i, n: (i, 0)),
            pl.BlockSpec((1, D_MODEL), lambda i, n: (0, 0)),
            pl.BlockSpec((D_MODEL, tn), lambda i, n: (0, n)),
            pl.BlockSpec((GATE_LANES, D_MODEL), lambda i, n: (0, 0)),
        ],
        out_specs=[
            pl.BlockSpec((tm, tn), lambda i, n: (i, n)),
            pl.BlockSpec((tm, GATE_LANES), lambda i, n: (i, 0)),
        ],
        out_shape=[
            jax.ShapeDtypeStruct((m, N_MAIN), F32),
            jax.ShapeDtypeStruct((m, GATE_LANES), F32),
        ],
        scratch_shapes=[pltpu.VMEM((tm, D_MODEL), BF16)],
        compiler_params=pltpu.CompilerParams(
            dimension_semantics=("arbitrary", "arbitrary"),
            vmem_limit_bytes=V7X_VMEM_LIMIT),
        name="inproj",
    )(x2d, g, w_main, w_gate)


def _attn_kernel(*refs, nq, gq, has_cache):
    if has_cache:
        (q_ref, ko_ref, vo_ref, z_ref, rb_ref, kc_ref, vc_ref,
         o_ref, nk_ref, nv_ref, kt, vcat, bias_scr) = refs
    else:
        (q_ref, ko_ref, vo_ref, z_ref, rb_ref,
         o_ref, nk_ref, nv_ref, kt, vcat, bias_scr) = refs
    t = pl.program_id(1)
    nt = pl.num_programs(1)
    rows = nq * CHUNK
    grows = gq * CHUNK
    gw = BAND_ROWS + grows

    @pl.when(jnp.logical_and(pl.program_id(0) == 0, t == 0))
    def _():
        lane = lax.broadcasted_iota(jnp.int32, (CHUNK, BIAS_LANES), 1)
        for h in range(HEADS):
            wide = jnp.broadcast_to(rb_ref[h:h + 1, :], (CHUNK, BIAS_LANES))
            for e in range(gq):
                shift = (BIAS_LANES - (CHUNK - 1) + e * CHUNK) % BIAS_LANES
                tab = pltpu.roll(wide, shift, 1, stride=1, stride_axis=0) * LOG2E
                tab = jnp.where(lane < e * CHUNK, NEG_INF, tab)
                tab = jnp.where(lane >= e * CHUNK + BAND_KEYS, NEG_INF, tab)
                bias_scr[h, e * CHUNK:(e + 1) * CHUNK, :] = tab

    eye16 = (lax.broadcasted_iota(jnp.int32, (HEAD_DIM, HEAD_DIM), 0)
             == lax.broadcasted_iota(jnp.int32, (HEAD_DIM, HEAD_DIM), 1)).astype(BF16)

    def transpose16(x16):
        return _dot_nt(eye16, x16).astype(BF16)

    if has_cache:
        for h in range(HEADS):
            sl = slice(h * HEAD_DIM, (h + 1) * HEAD_DIM)
            kt[h, :, 0:BAND_ROWS] = transpose16(kc_ref[:, h, :].astype(BF16))
            vcat[0:BAND_ROWS, sl] = vc_ref[:, h, :].astype(BF16)
    else:
        @pl.when(t == 0)
        def _():
            kt[:, :, 0:BAND_ROWS] = jnp.zeros((HEADS, HEAD_DIM, BAND_ROWS), BF16)
            vcat[0:BAND_ROWS, :] = jnp.zeros((BAND_ROWS, WIDTH), BF16)

        @pl.when(t > 0)
        def _():
            kt[:, :, 0:BAND_ROWS] = kt[:, :, rows:rows + BAND_ROWS]
            vcat[0:BAND_ROWS, :] = vcat[rows:rows + BAND_ROWS, :]
    for h in range(HEADS):
        sl = slice(h * HEAD_DIM, (h + 1) * HEAD_DIM)
        kt[h, :, BAND_ROWS:BAND_ROWS + rows] = transpose16(ko_ref[:, sl].astype(BF16))
    vcat[BAND_ROWS:BAND_ROWS + rows, :] = vo_ref[...].astype(BF16)

    col = lax.broadcasted_iota(jnp.int32, (grows, gw), 1)
    for h in range(HEADS):
        sl = slice(h * HEAD_DIM, (h + 1) * HEAD_DIM)
        bias = bias_scr[h][:, :gw]
        scores = []
        for g in range(nq // gq):
            r0 = g * grows
            q = (q_ref[r0:r0 + grows, sl] * (SCALE * LOG2E)).astype(BF16)
            scores.append(_dot(q, kt[h, :, r0:r0 + gw]))
        probs = []
        for g in range(nq // gq):
            s = scores[g] + bias
            if not has_cache:
                first_valid = jnp.where(t == 0, BAND_ROWS - g * grows, 0)
                s = jnp.where(col < first_valid, NEG_INF, s)
            m = jnp.max(s, axis=-1, keepdims=True)
            p = jnp.exp2(s - m)
            probs.append((p.astype(BF16), jnp.sum(p, axis=-1, keepdims=True)))
        outs = []
        for g in range(nq // gq):
            r0 = g * grows
            outs.append(_dot(probs[g][0], vcat[r0:r0 + gw, sl]))
        for g in range(nq // gq):
            r0 = g * grows
            o = outs[g] / probs[g][1]
            o_ref[r0:r0 + grows, sl] = (o * _silu(z_ref[r0:r0 + grows, sl])).astype(BF16)

    @pl.when(t == nt - 1)
    def _():
        if rows < BAND_ROWS:
            nk_ref[0:BAND_ROWS - rows] = kc_ref[rows:BAND_ROWS]
            nv_ref[0:BAND_ROWS - rows] = vc_ref[rows:BAND_ROWS]
        for h in range(HEADS):
            sl = slice(h * HEAD_DIM, (h + 1) * HEAD_DIM)
            nk_ref[BAND_ROWS - rows:BAND_ROWS, h, :] = ko_ref[:, sl]
            nv_ref[BAND_ROWS - rows:BAND_ROWS, h, :] = vo_ref[:, sl]


def _attn(p3, rev_bias, cache):
    b, t, _ = p3.shape
    has_cache = cache is not None
    rows = CHUNK if has_cache else BAND_ROWS
    assert t % rows == 0 and (not has_cache or t == CHUNK)
    nq = rows // CHUNK
    gq = min(nq, ATTN_GROUP_CHUNKS)
    assert BAND_ROWS + gq * CHUNK <= BIAS_LANES
    cur = lambda blk: pl.BlockSpec((None, rows, WIDTH), lambda i, tt: (i, tt, blk))
    newest = pl.BlockSpec((None, BAND_ROWS, HEADS, HEAD_DIM), lambda i, tt: (i, 0, 0, 0))
    in_specs = [cur(0), cur(1), cur(2), cur(3),
                pl.BlockSpec((HEADS, BIAS_LANES), lambda i, tt: (0, 0))]
    args = [p3, p3, p3, p3, rev_bias]
    if has_cache:
        ck, cv, layer = cache
        assert ck.shape[2:] == (BAND_ROWS, HEADS, HEAD_DIM)
        cached = pl.BlockSpec((None, None, BAND_ROWS, HEADS, HEAD_DIM),
                              lambda i, tt: (layer, i, 0, 0, 0))
        in_specs += [cached, cached]
        args += [ck, cv]
    return pl.pallas_call(
        functools.partial(_attn_kernel, nq=nq, gq=gq, has_cache=has_cache),
        grid=(b, t // rows),
        in_specs=in_specs,
        out_specs=[cur(0), newest, newest],
        out_shape=[jax.ShapeDtypeStruct((b, t, WIDTH), BF16),
                   jax.ShapeDtypeStruct((b, BAND_ROWS, HEADS, HEAD_DIM), F32),
                   jax.ShapeDtypeStruct((b, BAND_ROWS, HEADS, HEAD_DIM), F32)],
        scratch_shapes=[pltpu.VMEM((HEADS, HEAD_DIM, BAND_ROWS + rows), BF16),
                        pltpu.VMEM((BAND_ROWS + rows, WIDTH), BF16),
                        pltpu.VMEM((HEADS, gq * CHUNK, BIAS_LANES), F32)],
        compiler_params=pltpu.CompilerParams(
            dimension_semantics=("arbitrary", "arbitrary"), vmem_limit_bytes=V7X_VMEM_LIMIT),
        name="attn",
    )(*args)


def _delta_kernel(*refs, has_state, nch):
    if has_state:
        (q_ref, k_ref, v_ref, z_ref, gt_ref, cw_ref, al_ref, dt_ref, dn_ref, c0_ref, s0_ref,
         o_ref, sout_ref, tail_ref, xbuf, st) = refs
    else:
        (q_ref, k_ref, v_ref, z_ref, gt_ref, cw_ref, al_ref, dt_ref, dn_ref,
         o_ref, sout_ref, tail_ref, xbuf, st) = refs
    c = pl.program_id(1)
    nc = pl.num_programs(1)
    rows = nch * CHUNK

    lane_tiles = [(p, j, slice(p * WIDTH + j * HEAD_DIM, p * WIDTH + (j + 1) * HEAD_DIM))
                  for p in range(3) for j in range(HEADS)]

    @pl.when(c == 0)
    def _():
        if has_state:
            for p, j, cols in lane_tiles:
                xbuf[p, j, 0:8, :] = c0_ref[:, cols]
            st[...] = s0_ref[...]
        else:
            xbuf[:, :, 0:8, :] = jnp.zeros((3, HEADS, 8, HEAD_DIM), F32)
            st[...] = jnp.zeros_like(st)

    @pl.when(c > 0)
    def _():
        xbuf[:, :, 0:8, :] = xbuf[:, :, rows:rows + 8, :]

    for p, src in enumerate((q_ref, k_ref, v_ref)):
        for j in range(HEADS):
            xbuf[p, j, 8:8 + rows, :] = src[:, j * HEAD_DIM:(j + 1) * HEAD_DIM]

    def conv_act(p, h, cc):
        cols = slice(p * WIDTH + h * HEAD_DIM, p * WIDTH + (h + 1) * HEAD_DIM)
        y = None
        for i in range(CONV_W):
            start = 8 - (CONV_W - 1) + i + cc * CHUNK
            term = xbuf[p, h, start:start + CHUNK, :] * cw_ref[i:i + 1, cols]
            y = term if y is None else y + term
        return _silu(y)

    ri = lax.broadcasted_iota(jnp.int32, (CHUNK, CHUNK), 0)
    ci = lax.broadcasted_iota(jnp.int32, (CHUNK, CHUNK), 1)
    causal = ri >= ci
    strict = ri > ci
    ltri = causal.astype(F32)
    dn = dn_ref[...]

    chains = []
    for cc in range(nch):
        rs = slice(cc * CHUNK, (cc + 1) * CHUNK)
        gt = gt_ref[rs, :]
        beta_all = jax.nn.sigmoid(gt)
        g_all = -jnp.exp(al_ref[...]) * _softplus(gt + dt_ref[...])
        gcum = jnp.dot(ltri, g_all, precision=lax.Precision.HIGHEST,
                       preferred_element_type=F32)
        gcum_t = jnp.transpose(gcum)
        eg_all = jnp.exp(gcum)
        glast = gcum[CHUNK - 1:CHUNK, :]
        tail_all = jnp.exp(glast - gcum)
        gblk_all = jnp.exp(glast)
        for h in range(HEADS):
            sl = slice(h * HEAD_DIM, (h + 1) * HEAD_DIM)
            qh, kh, vh = conv_act(0, h, cc), conv_act(1, h, cc), conv_act(2, h, cc)
            qh = qh * lax.rsqrt(jnp.sum(qh * qh, axis=-1, keepdims=True) + EPS)
            kh = kh * lax.rsqrt(jnp.sum(kh * kh, axis=-1, keepdims=True) + EPS)
            beta = beta_all[:, h:h + 1]
            gl = HEADS + h
            eg = eg_all[:, gl:gl + 1]
            diff = gcum[:, gl:gl + 1] - gcum_t[gl:gl + 1, :]
            kbeta = kh * beta
            qs = qh * SCALE
            chains.append(dict(
                rs=rs, sl=sl, h=h,
                decay=jnp.exp(jnp.where(causal, diff, -jnp.inf)),
                k16=kh.astype(BF16), kbeta16=kbeta.astype(BF16), qs16=qs.astype(BF16),
                qdec=qs * eg,
                ktail16=(kh * tail_all[:, gl:gl + 1]).astype(BF16),
                gblk=gblk_all[:, gl:gl + 1],
                x=jnp.concatenate([vh * beta, kbeta * eg], axis=1)))

    for ch in chains:
        ch["a"] = jnp.where(strict, _dot_nt(ch["kbeta16"], ch["k16"]) * ch["decay"], 0.0)
    for ch in chains:
        ch["intra16"] = (_dot_nt(ch["qs16"], ch["k16"]) * ch["decay"]).astype(BF16)

    def same_block(size):
        shift = size.bit_length() - 1
        return jnp.right_shift(ri, shift) == jnp.right_shift(ci, shift)

    eye = (ri == ci).astype(F32)
    for ch in chains:
        n1 = jnp.where(same_block(8), -ch["a"], 0.0)
        ch["n1_16"] = n1.astype(BF16)
        ch["p"] = eye + n1
    for ch in chains:
        ch["n2"] = _dot(ch["n1_16"], ch["n1_16"])
    for ch in chains:
        ch["n2_16"] = ch["n2"].astype(BF16)
        ch["n4"] = _dot(ch["n2_16"], ch["n2_16"])
        ch["p"] = ch["p"] + _dot(ch["p"].astype(BF16), ch["n2_16"])
    for ch in chains:
        ch["t"] = ch["p"] + _dot(ch["p"].astype(BF16), ch["n4"].astype(BF16))
    for size in (8, 16, 32):
        couple = jnp.logical_and(same_block(2 * size), jnp.logical_not(same_block(size)))
        for ch in chains:
            ch["t16"] = ch["t"].astype(BF16)
            ch["te"] = _dot(ch["t16"], jnp.where(couple, ch["a"], 0.0).astype(BF16))
        for ch in chains:
            ch["t"] = ch["t"] - _dot(ch["te"].astype(BF16), ch["t16"])
    for ch in chains:
        ch["x"] = _dot(ch["t"].astype(BF16), ch["x"].astype(BF16))

    for ch in chains:
        x16 = ch["x"].astype(BF16)
        ix = _dot(ch["intra16"], x16)
        kx = _dot_tn(ch["ktail16"], x16)
        ch["o_intra"] = ix[:, :HEAD_DIM]
        ch["qeff16"] = (ch["qdec"] - ix[:, HEAD_DIM:]).astype(BF16)
        ch["b"] = kx[:, :HEAD_DIM]
        ch["p16"] = kx[:, HEAD_DIM:].astype(BF16)

    state = [st[h] for h in range(HEADS)]
    for cc in range(nch):
        for ch in chains[cc * HEADS:(cc + 1) * HEADS]:
            h, rs, sl = ch["h"], ch["rs"], ch["sl"]
            s16 = state[h].astype(BF16)
            o = _dot(ch["qeff16"], s16) + ch["o_intra"]
            state[h] = state[h] * ch["gblk"] - _dot(ch["p16"], s16) + ch["b"]
            o = o * lax.rsqrt(jnp.mean(o * o, axis=-1, keepdims=True) + EPS) * dn
            o_ref[rs, sl] = (o * _silu(z_ref[rs, sl])).astype(BF16)
    for h in range(HEADS):
        st[h] = state[h]

    @pl.when(c == nc - 1)
    def _():
        sout_ref[...] = st[...]
        for p, j, cols in lane_tiles:
            tail_ref[:, cols] = xbuf[p, j, rows:rows + 8, :]


def _delta(p3, gt3, conv_w, al_row, dt_row, dn_row, conv0, s0):
    b, t, _ = p3.shape
    has_state = s0 is not None
    nch = min(MIXER_CHUNKS_PER_STEP, t // CHUNK)
    rows = nch * CHUNK
    assert t % rows == 0
    col = lambda blk: pl.BlockSpec((None, rows, WIDTH), lambda i, c: (i, c, blk))
    full = lambda shape: pl.BlockSpec(shape, lambda i, c: (0,) * len(shape))
    in_specs = [col(4), col(5), col(6), col(7),
                pl.BlockSpec((None, rows, GATE_LANES), lambda i, c: (i, c, 0)),
                full((CONV_W, 3 * WIDTH)), full((1, GATE_LANES)), full((1, GATE_LANES)),
                full((1, HEAD_DIM))]
    args = [p3, p3, p3, p3, gt3, conv_w, al_row, dt_row, dn_row]
    if has_state:
        in_specs += [pl.BlockSpec((None, 8, 3 * WIDTH), lambda i, c: (i, 0, 0)),
                     pl.BlockSpec((None, HEADS, HEAD_DIM, HEAD_DIM), lambda i, c: (i, 0, 0, 0))]
        args += [conv0, s0]
    return pl.pallas_call(
        functools.partial(_delta_kernel, has_state=has_state, nch=nch),
        grid=(b, t // rows),
        in_specs=in_specs,
        out_specs=[pl.BlockSpec((None, rows, WIDTH), lambda i, c: (i, c, 0)),
                   pl.BlockSpec((None, HEADS, HEAD_DIM, HEAD_DIM), lambda i, c: (i, 0, 0, 0)),
                   pl.BlockSpec((None, 8, 3 * WIDTH), lambda i, c: (i, 0, 0))],
        out_shape=[jax.ShapeDtypeStruct((b, t, WIDTH), BF16),
                   jax.ShapeDtypeStruct((b, HEADS, HEAD_DIM, HEAD_DIM), F32),
                   jax.ShapeDtypeStruct((b, 8, 3 * WIDTH), F32)],
        scratch_shapes=[pltpu.VMEM((3, HEADS, rows + 8, HEAD_DIM), F32),
                        pltpu.VMEM((HEADS, HEAD_DIM, HEAD_DIM), F32)],
        compiler_params=pltpu.CompilerParams(
            dimension_semantics=("arbitrary", "arbitrary"), vmem_limit_bytes=V7X_VMEM_LIMIT),
        name="delta",
    )(*args)


def _ret_kernel(*refs, has_state, nch):
    if has_state:
        (q_ref, k_ref, v_ref, z_ref, cc_ref, ss_ref, dm_ref, xi_ref, zeta_ref, gb_ref, rn_ref,
         s0_ref, o_ref, sout_ref, st) = refs
    else:
        (q_ref, k_ref, v_ref, z_ref, cc_ref, ss_ref, dm_ref, xi_ref, zeta_ref, gb_ref, rn_ref,
         o_ref, sout_ref, st) = refs
    c = pl.program_id(1)
    nc = pl.num_programs(1)

    @pl.when(c == 0)
    def _():
        if has_state:
            st[...] = s0_ref[...]
        else:
            st[...] = jnp.zeros_like(st)

    chains = []
    for cc in range(nch):
        rs = slice(cc * CHUNK, (cc + 1) * CHUNK)
        cos2 = cc_ref[rs, :]
        sin2 = ss_ref[rs, :]
        for h in range(HEADS):
            sl = slice(h * HEAD_DIM, (h + 1) * HEAD_DIM)
            q = q_ref[rs, sl]
            k = k_ref[rs, sl]
            qh = q * cos2 + pltpu.roll(q, HEAD_DIM // 2, axis=1) * sin2
            kh = (k * cos2 + pltpu.roll(k, HEAD_DIM // 2, axis=1) * sin2) * SCALE
            chains.append(dict(rs=rs, sl=sl, h=h, q16=qh.astype(BF16), k16=kh.astype(BF16),
                               kz16=(kh * zeta_ref[h]).astype(BF16),
                               v16=v_ref[rs, sl].astype(BF16)))
    for ch in chains:
        ch["intra16"] = (_dot_nt(ch["q16"], ch["k16"]) * dm_ref[ch["h"]]).astype(BF16)
    for ch in chains:
        ch["kv"] = _dot_tn(ch["kz16"], ch["v16"])
    for ch in chains:
        ch["o_intra"] = _dot(ch["intra16"], ch["v16"])

    state = [st[h] for h in range(HEADS)]
    for cc in range(nch):
        for ch in chains[cc * HEADS:(cc + 1) * HEADS]:
            h, rs, sl = ch["h"], ch["rs"], ch["sl"]
            ch["o"] = ch["o_intra"] + _dot(ch["q16"], state[h].astype(BF16)) * xi_ref[h]
            state[h] = state[h] * gb_ref[h] + ch["kv"]
    for h in range(HEADS):
        st[h] = state[h]

    for ch in chains:
        ch["d"] = ch["o"] - jnp.mean(ch["o"], axis=-1, keepdims=True)
    for ch in chains:
        ch["var"] = jnp.mean(ch["d"] * ch["d"], axis=-1, keepdims=True)
    for ch in chains:
        rs, sl = ch["rs"], ch["sl"]
        o = ch["d"] * lax.rsqrt(ch["var"] + EPS) * rn_ref[:, sl]
        o_ref[rs, sl] = (o * _silu(z_ref[rs, sl])).astype(BF16)

    @pl.when(c == nc - 1)
    def _():
        sout_ref[...] = st[...]


def _ret(p3, cc, ss, consts, rn_row, s0):
    b, t, _ = p3.shape
    has_state = s0 is not None
    nch = min(RET_CHUNKS_PER_STEP, t // CHUNK)
    rows = nch * CHUNK
    assert t % rows == 0
    dmat, xi, zeta, gblk = consts
    col = lambda blk: pl.BlockSpec((None, rows, WIDTH), lambda i, c: (i, c, blk))
    full = lambda shape: pl.BlockSpec(shape, lambda i, c: (0,) * len(shape))
    tab = pl.BlockSpec((rows, HEAD_DIM), lambda i, c: (c, 0))
    in_specs = [col(8), col(9), col(10), col(11), tab, tab,
                full((HEADS, CHUNK, CHUNK)), full((HEADS, CHUNK, HEAD_DIM)),
                full((HEADS, CHUNK, HEAD_DIM)), full((HEADS, 1, HEAD_DIM)), full((1, WIDTH))]
    args = [p3, p3, p3, p3, cc, ss, dmat, xi, zeta, gblk, rn_row]
    if has_state:
        in_specs.append(pl.BlockSpec((None, HEADS, HEAD_DIM, HEAD_DIM), lambda i, c: (i, 0, 0, 0)))
        args.append(s0)
    return pl.pallas_call(
        functools.partial(_ret_kernel, has_state=has_state, nch=nch),
        grid=(b, t // rows),
        in_specs=in_specs,
        out_specs=[pl.BlockSpec((None, rows, WIDTH), lambda i, c: (i, c, 0)),
                   pl.BlockSpec((None, HEADS, HEAD_DIM, HEAD_DIM), lambda i, c: (i, 0, 0, 0))],
        out_shape=[jax.ShapeDtypeStruct((b, t, WIDTH), BF16),
                   jax.ShapeDtypeStruct((b, HEADS, HEAD_DIM, HEAD_DIM), F32)],
        scratch_shapes=[pltpu.VMEM((HEADS, HEAD_DIM, HEAD_DIM), F32)],
        compiler_params=pltpu.CompilerParams(
            dimension_semantics=("arbitrary", "arbitrary"), vmem_limit_bytes=V7X_VMEM_LIMIT),
        name="ret",
    )(*args)


def _merge_kernel(oa_ref, ob_ref, oc_ref, ga_ref, gb_ref, gc_ref, wa_ref, wb_ref, wc_ref,
                  wo_ref, x_ref, g_ref, o_ref):
    merged = (jax.nn.sigmoid(ga_ref[...]) * _dot(oa_ref[...], wa_ref[...])
              + jax.nn.sigmoid(gb_ref[...]) * _dot(ob_ref[...], wb_ref[...])
              + jax.nn.sigmoid(gc_ref[...]) * _dot(oc_ref[...], wc_ref[...]))
    y = _dot(merged.astype(BF16), wo_ref[...])
    y = y * lax.rsqrt(jnp.mean(y * y, axis=-1, keepdims=True) + EPS) * g_ref[...]
    o_ref[...] = x_ref[...] + y


def _merge(oa, ob, oc, p2, x2d, wa, wb, wc, wo, g):
    m = x2d.shape[0]
    tm = min(m, MERGE_ROWS)
    assert m % tm == 0
    row = lambda w: pl.BlockSpec((tm, w), lambda i: (i, 0))
    gate = lambda blk: pl.BlockSpec((tm, D_MODEL), lambda i: (i, blk))
    const = lambda shape: pl.BlockSpec(shape, lambda i: (0, 0), pipeline_mode=pl.Buffered(1))
    return pl.pallas_call(
        _merge_kernel,
        grid=(m // tm,),
        in_specs=[row(WIDTH), row(WIDTH), row(WIDTH), gate(6), gate(7), gate(8),
                  const((WIDTH, D_MODEL)), const((WIDTH, D_MODEL)), const((WIDTH, D_MODEL)),
                  const((D_MODEL, D_MODEL)), row(D_MODEL), const((1, D_MODEL))],
        out_specs=row(D_MODEL),
        out_shape=jax.ShapeDtypeStruct((m, D_MODEL), F32),
        compiler_params=pltpu.CompilerParams(
            dimension_semantics=("arbitrary",), vmem_limit_bytes=V7X_VMEM_LIMIT),
        name="merge",
    )(oa, ob, oc, p2, p2, p2, wa, wb, wc, wo, x2d, g)


def _retention_constants():
    log_gamma = jnp.log1p(-jnp.exp2(-5.0 - jnp.arange(HEADS, dtype=F32)))
    idx = jnp.arange(CHUNK, dtype=F32)
    rel = idx[:, None] - idx[None, :]
    lg = log_gamma[:, None, None]
    dmat = jnp.where(rel >= 0, jnp.exp(lg * jnp.maximum(rel, 0.0)), 0.0)
    xi = jnp.exp(log_gamma[:, None] * (idx + 1.0))
    zeta = jnp.exp(log_gamma[:, None] * (CHUNK - 1.0 - idx))
    gblk = jnp.exp(log_gamma * CHUNK)
    bc = lambda v: jnp.broadcast_to(v[:, :, None], (HEADS, v.shape[1], HEAD_DIM))
    return dmat, bc(xi), bc(zeta), bc(gblk[:, None])


def _rotary_tables(pos):
    half = HEAD_DIM // 2
    inv = ROPE_BASE ** (-jnp.arange(half, dtype=F32) / half)
    ang = pos.astype(F32)[:, None] * inv[None, :]
    cos, sin = jnp.cos(ang), jnp.sin(ang)
    return jnp.concatenate([cos, cos], axis=1), jnp.concatenate([-sin, sin], axis=1)


def _rev_bias(rel_bias):
    n_off = BAND_KEYS + CHUNK - 1
    head = rel_bias[:, MAX_REL - (CHUNK - 1):].astype(F32)
    edge = jnp.broadcast_to(head[:, -1:], (rel_bias.shape[0], n_off - head.shape[1]))
    ext = jnp.concatenate([head, edge], axis=1)
    return jnp.pad(ext[:, ::-1], ((0, 0), (0, BIAS_LANES - n_off)))


def _lane_row(v, lane0):
    return jnp.zeros((1, GATE_LANES), F32).at[0, lane0:lane0 + v.shape[0]].set(v.astype(F32))


def _group_layer(x, pos, cache, conv_buf, s_delta, s_ret, lw, ret_consts):
    (norm_pre, norm_post, w_main, w_gate, bias, conv_w, al_row, dt_row, dn_row, rn_row,
     wa, wb, wc, wo) = lw
    b, t, _ = x.shape
    x2d = x.reshape(b * t, D_MODEL)
    p2, gt2 = _inproj(x2d, norm_pre, w_main, w_gate)
    p3 = p2.reshape(b, t, N_MAIN)
    gt3 = gt2.reshape(b, t, GATE_LANES)

    oa, new_k, new_v = _attn(p3, bias, cache)

    conv0 = None if conv_buf is None else jnp.pad(conv_buf, ((0, 0), (8 - (CONV_W - 1), 0), (0, 0)))
    ob, new_sd, conv_tail = _delta(p3, gt3, conv_w, al_row, dt_row, dn_row, conv0, s_delta)
    new_conv = conv_tail[:, 8 - (CONV_W - 1):, :]

    cc, ss = _rotary_tables(pos)
    oc, new_sr = _ret(p3, cc, ss, ret_consts, rn_row, s_ret)

    y2d = _merge(oa.reshape(b * t, WIDTH), ob.reshape(b * t, WIDTH), oc.reshape(b * t, WIDTH),
                 p2, x2d, wa, wb, wc, wo, norm_post)
    return y2d.reshape(b, t, D_MODEL), new_k, new_v, new_conv, new_sd, new_sr


def kernel(x_prompt, x_sample, cache_attn_k, cache_attn_v, state_conv, state_delta, state_ret,
           norm_pre, norm_post, w_in, attn_rel_bias, conv_w, delta_a_log, delta_dt_bias,
           delta_norm, ret_norm, w_branch_a, w_branch_b, w_branch_c, w_out):
    depth = w_in.shape[0]
    tp, ts = x_prompt.shape[1], x_sample.shape[1]
    pos_p = jnp.arange(tp, dtype=jnp.int32)
    pos_s = PAST_LEN + jnp.arange(ts, dtype=jnp.int32)
    ret_consts = _retention_constants()
    w_in_t = jnp.swapaxes(w_in, 1, 2)
    xp, xs = x_prompt, x_sample
    acc_p = [[] for _ in range(5)]
    acc_s = [[] for _ in range(5)]
    for l in range(depth):
        w_main, w_gate = _weight_prep(w_in_t, l)
        lw = (norm_pre[l][None, :], norm_post[l][None, :], w_main, w_gate,
              _rev_bias(attn_rel_bias[l]), conv_w[l],
              _lane_row(delta_a_log[l], HEADS), _lane_row(delta_dt_bias[l], HEADS),
              delta_norm[l][None, :].astype(F32), ret_norm[l].reshape(1, WIDTH).astype(F32),
              w_branch_a[l].astype(BF16), w_branch_b[l].astype(BF16), w_branch_c[l].astype(BF16),
              w_out[l].astype(BF16))
        xp, *st_p = _group_layer(xp, pos_p, None, None, None, None, lw, ret_consts)
        xs, *st_s = _group_layer(xs, pos_s, (cache_attn_k, cache_attn_v, l), state_conv[l],
                                 state_delta[l], state_ret[l], lw, ret_consts)
        for acc, val in zip(acc_p, st_p):
            acc.append(val)
        for acc, val in zip(acc_s, st_s):
            acc.append(val)
    outs_p = [jnp.stack(a, axis=0) for a in acc_p]
    outs_s = [jnp.stack(a, axis=0) for a in acc_s]
    return (xp, xs, *outs_p, *outs_s)
```

```python
import functools

import jax
import jax.numpy as jnp
from jax import lax
from jax.experimental import pallas as pl
from jax.experimental.pallas import tpu as pltpu

F32 = jnp.float32
BF16 = jnp.bfloat16

D_MODEL = 2048
CHUNK = 64
HEAD_DIM = 128
HEADS = 8
WIDTH = HEADS * HEAD_DIM
BAND_CHUNKS = 8
BAND_ROWS = BAND_CHUNKS * CHUNK
BAND_KEYS = BAND_ROWS + CHUNK
BIAS_LANES = 640
MAX_REL = 128
CONV_W = 4
PAST_LEN = 2048
ROPE_BASE = 10000.0
EPS = 1e-6
NEG_INF = -1e30
SCALE = HEAD_DIM ** -0.5
LOG2E = 1.4426950408889634

N_MAIN = 18 * WIDTH
GATE_COL0 = 8 * WIDTH
GATE_LANES = 128
V7X_VMEM_LIMIT = 60 * 1024 * 1024
MIXER_CHUNKS_PER_STEP = 4
RET_CHUNKS_PER_STEP = 8
ATTN_GROUP_CHUNKS = 2
INPROJ_ROWS = 1024
INPROJ_COLS = 2048
MERGE_ROWS = 256
assert MAX_REL >= CHUNK - 1

_NT = (((1,), (1,)), ((), ()))
_TN = (((0,), (0,)), ((), ()))


def _dot(a, b):
    return jnp.dot(a, b, preferred_element_type=F32)


def _dot_nt(a, b):
    return lax.dot_general(a, b, _NT, preferred_element_type=F32)


def _dot_tn(a, b):
    return lax.dot_general(a, b, _TN, preferred_element_type=F32)


def _silu(x):
    return x * jax.nn.sigmoid(x)


def _softplus(x):
    return jnp.maximum(x, 0.0) + jnp.log1p(jnp.exp(-jnp.abs(x)))


PREP_COLS = 512
N_GATE = 2 * HEADS
assert GATE_COL0 % PREP_COLS == 0 and N_GATE % 8 == 0 and N_GATE <= GATE_LANES


def _weight_prep_kernel(a_ref, b_ref, o_ref, og_ref):
    n = pl.program_id(0)
    first_shifted = GATE_COL0 // PREP_COLS

    @pl.when(n < first_shifted)
    def _():
        o_ref[...] = jnp.transpose(a_ref[...]).astype(BF16)

    @pl.when(n >= first_shifted)
    def _():
        src = jnp.concatenate([a_ref[N_GATE:PREP_COLS, :], b_ref[...]], axis=0)
        o_ref[...] = jnp.transpose(src).astype(BF16)

    @pl.when(n == first_shifted)
    def _():
        pad = jnp.zeros((GATE_LANES - N_GATE, D_MODEL), F32)
        og_ref[...] = jnp.concatenate([a_ref[0:N_GATE, :], pad], axis=0).astype(BF16)


def _weight_prep(w_in_t, layer):
    assert w_in_t.shape[1] == N_MAIN + N_GATE
    return pl.pallas_call(
        _weight_prep_kernel,
        grid=(N_MAIN // PREP_COLS,),
        in_specs=[pl.BlockSpec((None, PREP_COLS, D_MODEL), lambda n: (layer, n, 0)),
                  pl.BlockSpec((None, N_GATE, D_MODEL),
                               lambda n: (layer, (n + 1) * (PREP_COLS // N_GATE), 0))],
        out_specs=[pl.BlockSpec((D_MODEL, PREP_COLS), lambda n: (0, n)),
                   pl.BlockSpec((GATE_LANES, D_MODEL), lambda n: (0, 0))],
        out_shape=[jax.ShapeDtypeStruct((D_MODEL, N_MAIN), BF16),
                   jax.ShapeDtypeStruct((GATE_LANES, D_MODEL), BF16)],
        compiler_params=pltpu.CompilerParams(
            dimension_semantics=("arbitrary",), vmem_limit_bytes=V7X_VMEM_LIMIT),
        name="wprep",
    )(w_in_t, w_in_t)


def _inproj_kernel(x_ref, g_ref, w_ref, wg_ref, o_ref, og_ref, h_ref):
    @pl.when(pl.program_id(1) == 0)
    def _():
        x = x_ref[...]
        ms = jnp.mean(x * x, axis=-1, keepdims=True)
        h = (x * lax.rsqrt(ms + EPS) * g_ref[...]).astype(BF16)
        h_ref[...] = h
        og_ref[...] = _dot_nt(h, wg_ref[...])

    o_ref[...] = _dot(h_ref[...], w_ref[...])


def _inproj(x2d, g, w_main, w_gate):
    m = x2d.shape[0]
    tm = min(m, INPROJ_ROWS)
    tn = INPROJ_COLS
    assert m % tm == 0 and N_MAIN % tn == 0
    return pl.pallas_call(
        _inproj_kernel,
        grid=(m // tm, N_MAIN // tn),
        in_specs=[
            pl.BlockSpec((tm, D_MODEL), lambda i, n: (i, 0)),
            pl.BlockSpec((1, D_MODEL), lambda i, n: (0, 0)),
            pl.BlockSpec((D_MODEL, tn), lambda i, n: (0, n)),
            pl.BlockSpec((GATE_LANES, D_MODEL), lambda i, n: (0, 0)),
        ],
        out_specs=[
            pl.BlockSpec((tm, tn), lambda i, n: (i, n)),
            pl.BlockSpec((tm, GATE_LANES), lambda i, n: (i, 0)),
        ],
        out_shape=[
            jax.ShapeDtypeStruct((m, N_MAIN), F32),
            jax.ShapeDtypeStruct((m, GATE_LANES), F32),
        ],
        scratch_shapes=[pltpu.VMEM((tm, D_MODEL), BF16)],
        compiler_params=pltpu.CompilerParams(
            dimension_semantics=("arbitrary", "arbitrary"),
            vmem_limit_bytes=V7X_VMEM_LIMIT),
        name="inproj",
    )(x2d, g, w_main, w_gate)


def _drop_aliased_inputs(refs, n_in, n_alias):
    return refs[:n_in] + refs[n_in + n_alias:]


def _stacked_out(prev, out_positions, in_specs, args):
    if prev is None:
        return {}
    aliases = {len(args) + k: pos for k, pos in enumerate(out_positions)}
    in_specs += [pl.BlockSpec(memory_space=pl.ANY)] * len(prev)
    args += list(prev)
    return aliases


def _attn_kernel(*refs, nq, gq, has_cache, n_alias):
    refs = _drop_aliased_inputs(refs, 7 if has_cache else 5, n_alias)
    if has_cache:
        (q_ref, ko_ref, vo_ref, z_ref, rb_ref, kc_ref, vc_ref,
         o_ref, nk_ref, nv_ref, kt, vcat, bias_scr) = refs
    else:
        (q_ref, ko_ref, vo_ref, z_ref, rb_ref,
         o_ref, nk_ref, nv_ref, kt, vcat, bias_scr) = refs
    t = pl.program_id(1)
    nt = pl.num_programs(1)
    rows = nq * CHUNK
    grows = gq * CHUNK
    gw = BAND_ROWS + grows

    @pl.when(jnp.logical_and(pl.program_id(0) == 0, t == 0))
    def _():
        lane = lax.broadcasted_iota(jnp.int32, (CHUNK, BIAS_LANES), 1)
        for h in range(HEADS):
            wide = jnp.broadcast_to(rb_ref[h:h + 1, :], (CHUNK, BIAS_LANES))
            for e in range(gq):
                shift = (BIAS_LANES - (CHUNK - 1) + e * CHUNK) % BIAS_LANES
                tab = pltpu.roll(wide, shift, 1, stride=1, stride_axis=0) * LOG2E
                tab = jnp.where(lane < e * CHUNK, NEG_INF, tab)
                tab = jnp.where(lane >= e * CHUNK + BAND_KEYS, NEG_INF, tab)
                bias_scr[h, e * CHUNK:(e + 1) * CHUNK, :] = tab

    eye16 = (lax.broadcasted_iota(jnp.int32, (HEAD_DIM, HEAD_DIM), 0)
             == lax.broadcasted_iota(jnp.int32, (HEAD_DIM, HEAD_DIM), 1)).astype(BF16)

    def transpose16(x16):
        return _dot_nt(eye16, x16).astype(BF16)

    if has_cache:
        for h in range(HEADS):
            sl = slice(h * HEAD_DIM, (h + 1) * HEAD_DIM)
            kt[h, :, 0:BAND_ROWS] = transpose16(kc_ref[:, h, :].astype(BF16))
            vcat[0:BAND_ROWS, sl] = vc_ref[:, h, :].astype(BF16)
    else:
        @pl.when(t == 0)
        def _():
            kt[:, :, 0:BAND_ROWS] = jnp.zeros((HEADS, HEAD_DIM, BAND_ROWS), BF16)
            vcat[0:BAND_ROWS, :] = jnp.zeros((BAND_ROWS, WIDTH), BF16)

        @pl.when(t > 0)
        def _():
            kt[:, :, 0:BAND_ROWS] = kt[:, :, rows:rows + BAND_ROWS]
            vcat[0:BAND_ROWS, :] = vcat[rows:rows + BAND_ROWS, :]
    for h in range(HEADS):
        sl = slice(h * HEAD_DIM, (h + 1) * HEAD_DIM)
        kt[h, :, BAND_ROWS:BAND_ROWS + rows] = transpose16(ko_ref[:, sl].astype(BF16))
    vcat[BAND_ROWS:BAND_ROWS + rows, :] = vo_ref[...].astype(BF16)

    col = lax.broadcasted_iota(jnp.int32, (grows, gw), 1)
    for h in range(HEADS):
        sl = slice(h * HEAD_DIM, (h + 1) * HEAD_DIM)
        bias = bias_scr[h][:, :gw]
        scores = []
        for g in range(nq // gq):
            r0 = g * grows
            q = (q_ref[r0:r0 + grows, sl] * (SCALE * LOG2E)).astype(BF16)
            scores.append(_dot(q, kt[h, :, r0:r0 + gw]))
        probs = []
        for g in range(nq // gq):
            s = scores[g] + bias
            if not has_cache:
                first_valid = jnp.where(t == 0, BAND_ROWS - g * grows, 0)
                s = jnp.where(col < first_valid, NEG_INF, s)
            m = jnp.max(s, axis=-1, keepdims=True)
            p = jnp.exp2(s - m)
            probs.append((p.astype(BF16), jnp.sum(p, axis=-1, keepdims=True)))
        outs = []
        for g in range(nq // gq):
            r0 = g * grows
            outs.append(_dot(probs[g][0], vcat[r0:r0 + gw, sl]))
        for g in range(nq // gq):
            r0 = g * grows
            o = outs[g] / probs[g][1]
            o_ref[r0:r0 + grows, sl] = (o * _silu(z_ref[r0:r0 + grows, sl])).astype(BF16)

    @pl.when(t == nt - 1)
    def _():
        if rows < BAND_ROWS:
            nk_ref[0:BAND_ROWS - rows] = kc_ref[rows:BAND_ROWS]
            nv_ref[0:BAND_ROWS - rows] = vc_ref[rows:BAND_ROWS]
        for h in range(HEADS):
            sl = slice(h * HEAD_DIM, (h + 1) * HEAD_DIM)
            nk_ref[BAND_ROWS - rows:BAND_ROWS, h, :] = ko_ref[:, sl]
            nv_ref[BAND_ROWS - rows:BAND_ROWS, h, :] = vo_ref[:, sl]


def _attn(p3, rev_bias, cache, layer, depth, prev):
    b, t, _ = p3.shape
    has_cache = cache is not None
    rows = CHUNK if has_cache else BAND_ROWS
    assert t % rows == 0 and (not has_cache or t == CHUNK)
    nq = rows // CHUNK
    gq = min(nq, ATTN_GROUP_CHUNKS)
    assert BAND_ROWS + gq * CHUNK <= BIAS_LANES
    cur = lambda blk: pl.BlockSpec((None, rows, WIDTH), lambda i, tt: (i, tt, blk))
    slab = pl.BlockSpec((None, None, BAND_ROWS, HEADS, HEAD_DIM),
                        lambda i, tt: (layer, i, 0, 0, 0))
    in_specs = [cur(0), cur(1), cur(2), cur(3),
                pl.BlockSpec((HEADS, BIAS_LANES), lambda i, tt: (0, 0))]
    args = [p3, p3, p3, p3, rev_bias]
    if has_cache:
        assert cache[0].shape[2:] == (BAND_ROWS, HEADS, HEAD_DIM)
        in_specs += [slab, slab]
        args += list(cache)
    aliases = _stacked_out(prev, (1, 2), in_specs, args)
    return pl.pallas_call(
        functools.partial(_attn_kernel, nq=nq, gq=gq, has_cache=has_cache, n_alias=len(aliases)),
        grid=(b, t // rows),
        in_specs=in_specs,
        out_specs=[cur(0), slab, slab],
        out_shape=[jax.ShapeDtypeStruct((b, t, WIDTH), BF16),
                   jax.ShapeDtypeStruct((depth, b, BAND_ROWS, HEADS, HEAD_DIM), F32),
                   jax.ShapeDtypeStruct((depth, b, BAND_ROWS, HEADS, HEAD_DIM), F32)],
        input_output_aliases=aliases,
        scratch_shapes=[pltpu.VMEM((HEADS, HEAD_DIM, BAND_ROWS + rows), BF16),
                        pltpu.VMEM((BAND_ROWS + rows, WIDTH), BF16),
                        pltpu.VMEM((HEADS, gq * CHUNK, BIAS_LANES), F32)],
        compiler_params=pltpu.CompilerParams(
            dimension_semantics=("arbitrary", "arbitrary"), vmem_limit_bytes=V7X_VMEM_LIMIT),
        name="attn",
    )(*args)


def _delta_kernel(*refs, has_state, nch, n_alias):
    refs = _drop_aliased_inputs(refs, 11 if has_state else 9, n_alias)
    if has_state:
        (q_ref, k_ref, v_ref, z_ref, gt_ref, cw_ref, al_ref, dt_ref, dn_ref, c0_ref, s0_ref,
         o_ref, sout_ref, tail_ref, xbuf, st) = refs
    else:
        (q_ref, k_ref, v_ref, z_ref, gt_ref, cw_ref, al_ref, dt_ref, dn_ref,
         o_ref, sout_ref, tail_ref, xbuf, st) = refs
    c = pl.program_id(1)
    nc = pl.num_programs(1)
    rows = nch * CHUNK

    lane_tiles = [(p, j, slice(p * WIDTH + j * HEAD_DIM, p * WIDTH + (j + 1) * HEAD_DIM))
                  for p in range(3) for j in range(HEADS)]

    @pl.when(c == 0)
    def _():
        if has_state:
            for p, j, cols in lane_tiles:
                xbuf[p, j, 0:8, :] = c0_ref[:, cols]
            st[...] = s0_ref[...]
        else:
            xbuf[:, :, 0:8, :] = jnp.zeros((3, HEADS, 8, HEAD_DIM), F32)
            st[...] = jnp.zeros_like(st)

    @pl.when(c > 0)
    def _():
        xbuf[:, :, 0:8, :] = xbuf[:, :, rows:rows + 8, :]

    for p, src in enumerate((q_ref, k_ref, v_ref)):
        for j in range(HEADS):
            xbuf[p, j, 8:8 + rows, :] = src[:, j * HEAD_DIM:(j + 1) * HEAD_DIM]

    def conv_act(p, h, cc):
        cols = slice(p * WIDTH + h * HEAD_DIM, p * WIDTH + (h + 1) * HEAD_DIM)
        y = None
        for i in range(CONV_W):
            start = 8 - (CONV_W - 1) + i + cc * CHUNK
            term = xbuf[p, h, start:start + CHUNK, :] * cw_ref[i:i + 1, cols]
            y = term if y is None else y + term
        return _silu(y)

    ri = lax.broadcasted_iota(jnp.int32, (CHUNK, CHUNK), 0)
    ci = lax.broadcasted_iota(jnp.int32, (CHUNK, CHUNK), 1)
    causal = ri >= ci
    strict = ri > ci
    ltri = causal.astype(F32)
    dn = dn_ref[...]

    chains = []
    for cc in range(nch):
        rs = slice(cc * CHUNK, (cc + 1) * CHUNK)
        gt = gt_ref[rs, :]
        beta_all = jax.nn.sigmoid(gt)
        g_all = -jnp.exp(al_ref[...]) * _softplus(gt + dt_ref[...])
        gcum = jnp.dot(ltri, g_all, precision=lax.Precision.HIGHEST,
                       preferred_element_type=F32)
        gcum_t = jnp.transpose(gcum)
        eg_all = jnp.exp(gcum)
        glast = gcum[CHUNK - 1:CHUNK, :]
        tail_all = jnp.exp(glast - gcum)
        gblk_all = jnp.exp(glast)
        for h in range(HEADS):
            sl = slice(h * HEAD_DIM, (h + 1) * HEAD_DIM)
            qh, kh, vh = conv_act(0, h, cc), conv_act(1, h, cc), conv_act(2, h, cc)
            qh = qh * lax.rsqrt(jnp.sum(qh * qh, axis=-1, keepdims=True) + EPS)
            kh = kh * lax.rsqrt(jnp.sum(kh * kh, axis=-1, keepdims=True) + EPS)
            beta = beta_all[:, h:h + 1]
            gl = HEADS + h
            eg = eg_all[:, gl:gl + 1]
            diff = gcum[:, gl:gl + 1] - gcum_t[gl:gl + 1, :]
            kbeta = kh * beta
            qs = qh * SCALE
            chains.append(dict(
                rs=rs, sl=sl, h=h,
                decay=jnp.exp(jnp.where(causal, diff, -jnp.inf)),
                k16=kh.astype(BF16), kbeta16=kbeta.astype(BF16), qs16=qs.astype(BF16),
                qdec=qs * eg,
                ktail16=(kh * tail_all[:, gl:gl + 1]).astype(BF16),
                gblk=gblk_all[:, gl:gl + 1],
                x=jnp.concatenate([vh * beta, kbeta * eg], axis=1)))

    for ch in chains:
        ch["a"] = jnp.where(strict, _dot_nt(ch["kbeta16"], ch["k16"]) * ch["decay"], 0.0)
    for ch in chains:
        ch["intra16"] = (_dot_nt(ch["qs16"], ch["k16"]) * ch["decay"]).astype(BF16)

    def same_block(size):
        shift = size.bit_length() - 1
        return jnp.right_shift(ri, shift) == jnp.right_shift(ci, shift)

    eye = (ri == ci).astype(F32)
    for ch in chains:
        n1 = jnp.where(same_block(8), -ch["a"], 0.0)
        ch["n1_16"] = n1.astype(BF16)
        ch["p"] = eye + n1
    for ch in chains:
        ch["n2"] = _dot(ch["n1_16"], ch["n1_16"])
    for ch in chains:
        ch["n2_16"] = ch["n2"].astype(BF16)
        ch["n4"] = _dot(ch["n2_16"], ch["n2_16"])
        ch["p"] = ch["p"] + _dot(ch["p"].astype(BF16), ch["n2_16"])
    for ch in chains:
        ch["t"] = ch["p"] + _dot(ch["p"].astype(BF16), ch["n4"].astype(BF16))
    for size in (8, 16, 32):
        couple = jnp.logical_and(same_block(2 * size), jnp.logical_not(same_block(size)))
        for ch in chains:
            ch["t16"] = ch["t"].astype(BF16)
            ch["te"] = _dot(ch["t16"], jnp.where(couple, ch["a"], 0.0).astype(BF16))
        for ch in chains:
            ch["t"] = ch["t"] - _dot(ch["te"].astype(BF16), ch["t16"])
    for ch in chains:
        ch["x"] = _dot(ch["t"].astype(BF16), ch["x"].astype(BF16))

    for ch in chains:
        x16 = ch["x"].astype(BF16)
        ix = _dot(ch["intra16"], x16)
        kx = _dot_tn(ch["ktail16"], x16)
        ch["o_intra"] = ix[:, :HEAD_DIM]
        ch["qeff16"] = (ch["qdec"] - ix[:, HEAD_DIM:]).astype(BF16)
        ch["b"] = kx[:, :HEAD_DIM]
        ch["p16"] = kx[:, HEAD_DIM:].astype(BF16)

    state = [st[h] for h in range(HEADS)]
    for cc in range(nch):
        for ch in chains[cc * HEADS:(cc + 1) * HEADS]:
            h, rs, sl = ch["h"], ch["rs"], ch["sl"]
            s16 = state[h].astype(BF16)
            o = _dot(ch["qeff16"], s16) + ch["o_intra"]
            state[h] = state[h] * ch["gblk"] - _dot(ch["p16"], s16) + ch["b"]
            o = o * lax.rsqrt(jnp.mean(o * o, axis=-1, keepdims=True) + EPS) * dn
            o_ref[rs, sl] = (o * _silu(z_ref[rs, sl])).astype(BF16)
    for h in range(HEADS):
        st[h] = state[h]

    @pl.when(c == nc - 1)
    def _():
        sout_ref[...] = st[...]
        for p, j, cols in lane_tiles:
            tail_ref[:, cols] = xbuf[p, j, rows:rows + 8, :]


def _delta(p3, gt3, conv_w, al_row, dt_row, dn_row, conv0, s0, layer, depth, prev):
    b, t, _ = p3.shape
    has_state = s0 is not None
    nch = min(MIXER_CHUNKS_PER_STEP, t // CHUNK)
    rows = nch * CHUNK
    assert t % rows == 0
    col = lambda blk: pl.BlockSpec((None, rows, WIDTH), lambda i, c: (i, c, blk))
    full = lambda shape: pl.BlockSpec(shape, lambda i, c: (0,) * len(shape))
    in_specs = [col(4), col(5), col(6), col(7),
                pl.BlockSpec((None, rows, GATE_LANES), lambda i, c: (i, c, 0)),
                full((CONV_W, 3 * WIDTH)), full((1, GATE_LANES)), full((1, GATE_LANES)),
                full((1, HEAD_DIM))]
    args = [p3, p3, p3, p3, gt3, conv_w, al_row, dt_row, dn_row]
    if has_state:
        in_specs += [pl.BlockSpec((None, 8, 3 * WIDTH), lambda i, c: (i, 0, 0)),
                     pl.BlockSpec((None, HEADS, HEAD_DIM, HEAD_DIM), lambda i, c: (i, 0, 0, 0))]
        args += [conv0, s0]
    aliases = _stacked_out(prev, (1, 2), in_specs, args)
    return pl.pallas_call(
        functools.partial(_delta_kernel, has_state=has_state, nch=nch, n_alias=len(aliases)),
        grid=(b, t // rows),
        in_specs=in_specs,
        out_specs=[pl.BlockSpec((None, rows, WIDTH), lambda i, c: (i, c, 0)),
                   pl.BlockSpec((None, None, HEADS, HEAD_DIM, HEAD_DIM),
                                lambda i, c: (layer, i, 0, 0, 0)),
                   pl.BlockSpec((None, None, 8, 3 * WIDTH), lambda i, c: (layer, i, 0, 0))],
        out_shape=[jax.ShapeDtypeStruct((b, t, WIDTH), BF16),
                   jax.ShapeDtypeStruct((depth, b, HEADS, HEAD_DIM, HEAD_DIM), F32),
                   jax.ShapeDtypeStruct((depth, b, 8, 3 * WIDTH), F32)],
        input_output_aliases=aliases,
        scratch_shapes=[pltpu.VMEM((3, HEADS, rows + 8, HEAD_DIM), F32),
                        pltpu.VMEM((HEADS, HEAD_DIM, HEAD_DIM), F32)],
        compiler_params=pltpu.CompilerParams(
            dimension_semantics=("arbitrary", "arbitrary"), vmem_limit_bytes=V7X_VMEM_LIMIT),
        name="delta",
    )(*args)


def _ret_kernel(*refs, has_state, nch, n_alias):
    refs = _drop_aliased_inputs(refs, 12 if has_state else 11, n_alias)
    if has_state:
        (q_ref, k_ref, v_ref, z_ref, cc_ref, ss_ref, dm_ref, xi_ref, zeta_ref, gb_ref, rn_ref,
         s0_ref, o_ref, sout_ref, st) = refs
    else:
        (q_ref, k_ref, v_ref, z_ref, cc_ref, ss_ref, dm_ref, xi_ref, zeta_ref, gb_ref, rn_ref,
         o_ref, sout_ref, st) = refs
    c = pl.program_id(1)
    nc = pl.num_programs(1)

    @pl.when(c == 0)
    def _():
        if has_state:
            st[...] = s0_ref[...]
        else:
            st[...] = jnp.zeros_like(st)

    chains = []
    for cc in range(nch):
        rs = slice(cc * CHUNK, (cc + 1) * CHUNK)
        cos2 = cc_ref[rs, :]
        sin2 = ss_ref[rs, :]
        for h in range(HEADS):
            sl = slice(h * HEAD_DIM, (h + 1) * HEAD_DIM)
            q = q_ref[rs, sl]
            k = k_ref[rs, sl]
            qh = q * cos2 + pltpu.roll(q, HEAD_DIM // 2, axis=1) * sin2
            kh = (k * cos2 + pltpu.roll(k, HEAD_DIM // 2, axis=1) * sin2) * SCALE
            chains.append(dict(rs=rs, sl=sl, h=h, q16=qh.astype(BF16), k16=kh.astype(BF16),
                               kz16=(kh * zeta_ref[h]).astype(BF16),
                               v16=v_ref[rs, sl].astype(BF16)))
    for ch in chains:
        ch["intra16"] = (_dot_nt(ch["q16"], ch["k16"]) * dm_ref[ch["h"]]).astype(BF16)
    for ch in chains:
        ch["kv"] = _dot_tn(ch["kz16"], ch["v16"])
    for ch in chains:
        ch["o_intra"] = _dot(ch["intra16"], ch["v16"])

    state = [st[h] for h in range(HEADS)]
    for cc in range(nch):
        for ch in chains[cc * HEADS:(cc + 1) * HEADS]:
            h, rs, sl = ch["h"], ch["rs"], ch["sl"]
            ch["o"] = ch["o_intra"] + _dot(ch["q16"], state[h].astype(BF16)) * xi_ref[h]
            state[h] = state[h] * gb_ref[h] + ch["kv"]
    for h in range(HEADS):
        st[h] = state[h]

    for ch in chains:
        ch["d"] = ch["o"] - jnp.mean(ch["o"], axis=-1, keepdims=True)
    for ch in chains:
        ch["var"] = jnp.mean(ch["d"] * ch["d"], axis=-1, keepdims=True)
    for ch in chains:
        rs, sl = ch["rs"], ch["sl"]
        o = ch["d"] * lax.rsqrt(ch["var"] + EPS) * rn_ref[:, sl]
        o_ref[rs, sl] = (o * _silu(z_ref[rs, sl])).astype(BF16)

    @pl.when(c == nc - 1)
    def _():
        sout_ref[...] = st[...]


def _ret(p3, cc, ss, consts, rn_row, s0, layer, depth, prev):
    b, t, _ = p3.shape
    has_state = s0 is not None
    nch = min(RET_CHUNKS_PER_STEP, t // CHUNK)
    rows = nch * CHUNK
    assert t % rows == 0
    dmat, xi, zeta, gblk = consts
    col = lambda blk: pl.BlockSpec((None, rows, WIDTH), lambda i, c: (i, c, blk))
    full = lambda shape: pl.BlockSpec(shape, lambda i, c: (0,) * len(shape))
    tab = pl.BlockSpec((rows, HEAD_DIM), lambda i, c: (c, 0))
    in_specs = [col(8), col(9), col(10), col(11), tab, tab,
                full((HEADS, CHUNK, CHUNK)), full((HEADS, CHUNK, HEAD_DIM)),
                full((HEADS, CHUNK, HEAD_DIM)), full((HEADS, 1, HEAD_DIM)), full((1, WIDTH))]
    args = [p3, p3, p3, p3, cc, ss, dmat, xi, zeta, gblk, rn_row]
    if has_state:
        in_specs.append(pl.BlockSpec((None, HEADS, HEAD_DIM, HEAD_DIM), lambda i, c: (i, 0, 0, 0)))
        args.append(s0)
    aliases = _stacked_out(prev, (1,), in_specs, args)
    return pl.pallas_call(
        functools.partial(_ret_kernel, has_state=has_state, nch=nch, n_alias=len(aliases)),
        grid=(b, t // rows),
        in_specs=in_specs,
        out_specs=[pl.BlockSpec((None, rows, WIDTH), lambda i, c: (i, c, 0)),
                   pl.BlockSpec((None, None, HEADS, HEAD_DIM, HEAD_DIM),
                                lambda i, c: (layer, i, 0, 0, 0))],
        out_shape=[jax.ShapeDtypeStruct((b, t, WIDTH), BF16),
                   jax.ShapeDtypeStruct((depth, b, HEADS, HEAD_DIM, HEAD_DIM), F32)],
        input_output_aliases=aliases,
        scratch_shapes=[pltpu.VMEM((HEADS, HEAD_DIM, HEAD_DIM), F32)],
        compiler_params=pltpu.CompilerParams(
            dimension_semantics=("arbitrary", "arbitrary"), vmem_limit_bytes=V7X_VMEM_LIMIT),
        name="ret",
    )(*args)


def _merge_kernel(oa_ref, ob_ref, oc_ref, ga_ref, gb_ref, gc_ref, wa_ref, wb_ref, wc_ref,
                  wo_ref, x_ref, g_ref, o_ref):
    merged = (jax.nn.sigmoid(ga_ref[...]) * _dot(oa_ref[...], wa_ref[...])
              + jax.nn.sigmoid(gb_ref[...]) * _dot(ob_ref[...], wb_ref[...])
              + jax.nn.sigmoid(gc_ref[...]) * _dot(oc_ref[...], wc_ref[...]))
    y = _dot(merged.astype(BF16), wo_ref[...])
    y = y * lax.rsqrt(jnp.mean(y * y, axis=-1, keepdims=True) + EPS) * g_ref[...]
    o_ref[...] = x_ref[...] + y


def _merge(oa, ob, oc, p2, x2d, wa, wb, wc, wo, g):
    m = x2d.shape[0]
    tm = min(m, MERGE_ROWS)
    assert m % tm == 0
    row = lambda w: pl.BlockSpec((tm, w), lambda i: (i, 0))
    gate = lambda blk: pl.BlockSpec((tm, D_MODEL), lambda i: (i, blk))
    const = lambda shape: pl.BlockSpec(shape, lambda i: (0, 0), pipeline_mode=pl.Buffered(1))
    return pl.pallas_call(
        _merge_kernel,
        grid=(m // tm,),
        in_specs=[row(WIDTH), row(WIDTH), row(WIDTH), gate(6), gate(7), gate(8),
                  const((WIDTH, D_MODEL)), const((WIDTH, D_MODEL)), const((WIDTH, D_MODEL)),
                  const((D_MODEL, D_MODEL)), row(D_MODEL), const((1, D_MODEL))],
        out_specs=row(D_MODEL),
        out_shape=jax.ShapeDtypeStruct((m, D_MODEL), F32),
        compiler_params=pltpu.CompilerParams(
            dimension_semantics=("arbitrary",), vmem_limit_bytes=V7X_VMEM_LIMIT),
        name="merge",
    )(oa, ob, oc, p2, p2, p2, wa, wb, wc, wo, x2d, g)


def _retention_constants():
    log_gamma = jnp.log1p(-jnp.exp2(-5.0 - jnp.arange(HEADS, dtype=F32)))
    idx = jnp.arange(CHUNK, dtype=F32)
    rel = idx[:, None] - idx[None, :]
    lg = log_gamma[:, None, None]
    dmat = jnp.where(rel >= 0, jnp.exp(lg * jnp.maximum(rel, 0.0)), 0.0)
    xi = jnp.exp(log_gamma[:, None] * (idx + 1.0))
    zeta = jnp.exp(log_gamma[:, None] * (CHUNK - 1.0 - idx))
    gblk = jnp.exp(log_gamma * CHUNK)
    bc = lambda v: jnp.broadcast_to(v[:, :, None], (HEADS, v.shape[1], HEAD_DIM))
    return dmat, bc(xi), bc(zeta), bc(gblk[:, None])


def _rotary_tables(pos):
    half = HEAD_DIM // 2
    inv = ROPE_BASE ** (-jnp.arange(half, dtype=F32) / half)
    ang = pos.astype(F32)[:, None] * inv[None, :]
    cos, sin = jnp.cos(ang), jnp.sin(ang)
    return jnp.concatenate([cos, cos], axis=1), jnp.concatenate([-sin, sin], axis=1)


def _rev_bias(rel_bias):
    n_off = BAND_KEYS + CHUNK - 1
    head = rel_bias[:, MAX_REL - (CHUNK - 1):].astype(F32)
    edge = jnp.broadcast_to(head[:, -1:], (rel_bias.shape[0], n_off - head.shape[1]))
    ext = jnp.concatenate([head, edge], axis=1)
    return jnp.pad(ext[:, ::-1], ((0, 0), (0, BIAS_LANES - n_off)))


def _lane_row(v, lane0):
    return jnp.zeros((1, GATE_LANES), F32).at[0, lane0:lane0 + v.shape[0]].set(v.astype(F32))


def _group_layer(x, pos, cache, conv_buf, s_delta, s_ret, lw, ret_consts, layer, depth, prev):
    (norm_pre, norm_post, w_main, w_gate, bias, conv_w, al_row, dt_row, dn_row, rn_row,
     wa, wb, wc, wo) = lw
    b, t, _ = x.shape
    x2d = x.reshape(b * t, D_MODEL)
    p2, gt2 = _inproj(x2d, norm_pre, w_main, w_gate)
    p3 = p2.reshape(b, t, N_MAIN)
    gt3 = gt2.reshape(b, t, GATE_LANES)
    prev_k, prev_v, prev_sd, prev_tail, prev_sr = prev if prev is not None else (None,) * 5
    pack = lambda *arrays: None if prev is None else arrays

    oa, new_k, new_v = _attn(p3, bias, cache, layer, depth, pack(prev_k, prev_v))

    conv0 = None if conv_buf is None else jnp.pad(conv_buf, ((0, 0), (8 - (CONV_W - 1), 0), (0, 0)))
    ob, new_sd, conv_tail = _delta(p3, gt3, conv_w, al_row, dt_row, dn_row, conv0, s_delta,
                                   layer, depth, pack(prev_sd, prev_tail))

    cc, ss = _rotary_tables(pos)
    oc, new_sr = _ret(p3, cc, ss, ret_consts, rn_row, s_ret, layer, depth, pack(prev_sr))

    y2d = _merge(oa.reshape(b * t, WIDTH), ob.reshape(b * t, WIDTH), oc.reshape(b * t, WIDTH),
                 p2, x2d, wa, wb, wc, wo, norm_post)
    return y2d.reshape(b, t, D_MODEL), (new_k, new_v, new_sd, conv_tail, new_sr)


def kernel(x_prompt, x_sample, cache_attn_k, cache_attn_v, state_conv, state_delta, state_ret,
           norm_pre, norm_post, w_in, attn_rel_bias, conv_w, delta_a_log, delta_dt_bias,
           delta_norm, ret_norm, w_branch_a, w_branch_b, w_branch_c, w_out):
    depth = w_in.shape[0]
    tp, ts = x_prompt.shape[1], x_sample.shape[1]
    pos_p = jnp.arange(tp, dtype=jnp.int32)
    pos_s = PAST_LEN + jnp.arange(ts, dtype=jnp.int32)
    ret_consts = _retention_constants()
    w_in_t = jnp.swapaxes(w_in, 1, 2)
    xp, xs = x_prompt, x_sample
    st_p = st_s = None
    for l in range(depth):
        w_main, w_gate = _weight_prep(w_in_t, l)
        lw = (norm_pre[l][None, :], norm_post[l][None, :], w_main, w_gate,
              _rev_bias(attn_rel_bias[l]), conv_w[l],
              _lane_row(delta_a_log[l], HEADS), _lane_row(delta_dt_bias[l], HEADS),
              delta_norm[l][None, :].astype(F32), ret_norm[l].reshape(1, WIDTH).astype(F32),
              w_branch_a[l].astype(BF16), w_branch_b[l].astype(BF16), w_branch_c[l].astype(BF16),
              w_out[l].astype(BF16))
        xp, st_p = _group_layer(xp, pos_p, None, None, None, None, lw, ret_consts,
                                l, depth, st_p)
        xs, st_s = _group_layer(xs, pos_s, (cache_attn_k, cache_attn_v), state_conv[l],
                                state_delta[l], state_ret[l], lw, ret_consts, l, depth, st_s)

    def finish(stacked):
        new_k, new_v, new_sd, conv_tail, new_sr = stacked
        return new_k, new_v, conv_tail[:, :, 8 - (CONV_W - 1):, :], new_sd, new_sr

    return (xp, xs, *finish(st_p), *finish(st_s))
```

```python
import functools

import jax
import jax.numpy as jnp
from jax import lax
from jax.experimental import pallas as pl
from jax.experimental.pallas import tpu as pltpu

F32 = jnp.float32
BF16 = jnp.bfloat16

D_MODEL = 2048
CHUNK = 64
HEAD_DIM = 128
HEADS = 8
WIDTH = HEADS * HEAD_DIM
BAND_CHUNKS = 8
BAND_ROWS = BAND_CHUNKS * CHUNK
BAND_KEYS = BAND_ROWS + CHUNK
BIAS_LANES = 640
MAX_REL = 128
CONV_W = 4
PAST_LEN = 2048
ROPE_BASE = 10000.0
EPS = 1e-6
NEG_INF = -1e30
SCALE = HEAD_DIM ** -0.5
LOG2E = 1.4426950408889634

N_MAIN = 18 * WIDTH
GATE_COL0 = 8 * WIDTH
GATE_LANES = 128
V7X_VMEM_LIMIT = 60 * 1024 * 1024
MIXER_CHUNKS_PER_STEP = 4
RET_CHUNKS_PER_STEP = 8
ATTN_GROUP_CHUNKS = 2
INPROJ_ROWS = 1024
INPROJ_COLS = 2048
MERGE_ROWS = 256
assert MAX_REL >= CHUNK - 1

_NT = (((1,), (1,)), ((), ()))
_TN = (((0,), (0,)), ((), ()))


def _dot(a, b):
    return jnp.dot(a, b, preferred_element_type=F32)


def _dot_nt(a, b):
    return lax.dot_general(a, b, _NT, preferred_element_type=F32)


def _dot_tn(a, b):
    return lax.dot_general(a, b, _TN, preferred_element_type=F32)


def _silu(x):
    return x * jax.nn.sigmoid(x)


def _softplus(x):
    return jnp.maximum(x, 0.0) + jnp.log1p(jnp.exp(-jnp.abs(x)))


PREP_COLS = 512
N_GATE = 2 * HEADS
assert GATE_COL0 % PREP_COLS == 0 and N_GATE % 8 == 0 and N_GATE <= GATE_LANES


def _weight_prep_kernel(a_ref, b_ref, o_ref, og_ref):
    n = pl.program_id(0)
    first_shifted = GATE_COL0 // PREP_COLS

    @pl.when(n < first_shifted)
    def _():
        o_ref[...] = jnp.transpose(a_ref[...]).astype(BF16)

    @pl.when(n >= first_shifted)
    def _():
        src = jnp.concatenate([a_ref[N_GATE:PREP_COLS, :], b_ref[...]], axis=0)
        o_ref[...] = jnp.transpose(src).astype(BF16)

    @pl.when(n == first_shifted)
    def _():
        pad = jnp.zeros((GATE_LANES - N_GATE, D_MODEL), F32)
        og_ref[...] = jnp.concatenate([a_ref[0:N_GATE, :], pad], axis=0).astype(BF16)


def _weight_prep(w_in_t, layer):
    assert w_in_t.shape[1] == N_MAIN + N_GATE
    return pl.pallas_call(
        _weight_prep_kernel,
        grid=(N_MAIN // PREP_COLS,),
        in_specs=[pl.BlockSpec((None, PREP_COLS, D_MODEL), lambda n: (layer, n, 0)),
                  pl.BlockSpec((None, N_GATE, D_MODEL),
                               lambda n: (layer, (n + 1) * (PREP_COLS // N_GATE), 0))],
        out_specs=[pl.BlockSpec((D_MODEL, PREP_COLS), lambda n: (0, n)),
                   pl.BlockSpec((GATE_LANES, D_MODEL), lambda n: (0, 0))],
        out_shape=[jax.ShapeDtypeStruct((D_MODEL, N_MAIN), BF16),
                   jax.ShapeDtypeStruct((GATE_LANES, D_MODEL), BF16)],
        compiler_params=pltpu.CompilerParams(
            dimension_semantics=("arbitrary",), vmem_limit_bytes=V7X_VMEM_LIMIT),
        name="wprep",
    )(w_in_t, w_in_t)


def _inproj_kernel(x_ref, g_ref, w_ref, wg_ref, o_ref, og_ref, h_ref):
    @pl.when(pl.program_id(1) == 0)
    def _():
        x = x_ref[...]
        ms = jnp.mean(x * x, axis=-1, keepdims=True)
        h = (x * lax.rsqrt(ms + EPS) * g_ref[...]).astype(BF16)
        h_ref[...] = h
        og_ref[...] = _dot_nt(h, wg_ref[...])

    o_ref[...] = _dot(h_ref[...], w_ref[...])


def _inproj(x2d, g, w_main, w_gate):
    m = x2d.shape[0]
    tm = min(m, INPROJ_ROWS)
    tn = INPROJ_COLS
    assert m % tm == 0 and N_MAIN % tn == 0
    return pl.pallas_call(
        _inproj_kernel,
        grid=(m // tm, N_MAIN // tn),
        in_specs=[
            pl.BlockSpec((tm, D_MODEL), lambda i, n: (i, 0)),
            pl.BlockSpec((1, D_MODEL), lambda i, n: (0, 0)),
            pl.BlockSpec((D_MODEL, tn), lambda i, n: (0, n)),
            pl.BlockSpec((GATE_LANES, D_MODEL), lambda i, n: (0, 0)),
        ],
        out_specs=[
            pl.BlockSpec((tm, tn), lambda i, n: (i, n)),
            pl.BlockSpec((tm, GATE_LANES), lambda i, n: (i, 0)),
        ],
        out_shape=[
            jax.ShapeDtypeStruct((m, N_MAIN), F32),
            jax.ShapeDtypeStruct((m, GATE_LANES), F32),
        ],
        scratch_shapes=[pltpu.VMEM((tm, D_MODEL), BF16)],
        compiler_params=pltpu.CompilerParams(
            dimension_semantics=("arbitrary", "arbitrary"),
            vmem_limit_bytes=V7X_VMEM_LIMIT),
        name="inproj",
    )(x2d, g, w_main, w_gate)


def _drop_aliased_inputs(refs, n_in, n_alias):
    return refs[:n_in] + refs[n_in + n_alias:]


def _stacked_out(prev, out_positions, in_specs, args):
    aliases = {len(args) + k: pos for k, pos in enumerate(out_positions)}
    in_specs += [pl.BlockSpec(memory_space=pl.ANY)] * len(prev)
    args += list(prev)
    return aliases


def _attn_kernel(*refs, nq, gq, has_cache, n_alias):
    refs = _drop_aliased_inputs(refs, 7 if has_cache else 5, n_alias)
    if has_cache:
        (q_ref, ko_ref, vo_ref, z_ref, rb_ref, kc_ref, vc_ref,
         o_ref, nk_ref, nv_ref, kt, vcat, bias_scr) = refs
    else:
        (q_ref, ko_ref, vo_ref, z_ref, rb_ref,
         o_ref, nk_ref, nv_ref, kt, vcat, bias_scr) = refs
    t = pl.program_id(1)
    nt = pl.num_programs(1)
    rows = nq * CHUNK
    grows = gq * CHUNK
    gw = BAND_ROWS + grows

    @pl.when(jnp.logical_and(pl.program_id(0) == 0, t == 0))
    def _():
        lane = lax.broadcasted_iota(jnp.int32, (CHUNK, BIAS_LANES), 1)
        for h in range(HEADS):
            wide = jnp.broadcast_to(rb_ref[h:h + 1, :], (CHUNK, BIAS_LANES))
            for e in range(gq):
                shift = (BIAS_LANES - (CHUNK - 1) + e * CHUNK) % BIAS_LANES
                tab = pltpu.roll(wide, shift, 1, stride=1, stride_axis=0) * LOG2E
                tab = jnp.where(lane < e * CHUNK, NEG_INF, tab)
                tab = jnp.where(lane >= e * CHUNK + BAND_KEYS, NEG_INF, tab)
                bias_scr[h, e * CHUNK:(e + 1) * CHUNK, :] = tab

    eye16 = (lax.broadcasted_iota(jnp.int32, (HEAD_DIM, HEAD_DIM), 0)
             == lax.broadcasted_iota(jnp.int32, (HEAD_DIM, HEAD_DIM), 1)).astype(BF16)

    def transpose16(x16):
        return _dot_nt(eye16, x16).astype(BF16)

    if has_cache:
        for h in range(HEADS):
            sl = slice(h * HEAD_DIM, (h + 1) * HEAD_DIM)
            kt[h, :, 0:BAND_ROWS] = transpose16(kc_ref[:, h, :].astype(BF16))
            vcat[0:BAND_ROWS, sl] = vc_ref[:, h, :].astype(BF16)
    else:
        @pl.when(t == 0)
        def _():
            kt[:, :, 0:BAND_ROWS] = jnp.zeros((HEADS, HEAD_DIM, BAND_ROWS), BF16)
            vcat[0:BAND_ROWS, :] = jnp.zeros((BAND_ROWS, WIDTH), BF16)

        @pl.when(t > 0)
        def _():
            kt[:, :, 0:BAND_ROWS] = kt[:, :, rows:rows + BAND_ROWS]
            vcat[0:BAND_ROWS, :] = vcat[rows:rows + BAND_ROWS, :]
    for h in range(HEADS):
        sl = slice(h * HEAD_DIM, (h + 1) * HEAD_DIM)
        kt[h, :, BAND_ROWS:BAND_ROWS + rows] = transpose16(ko_ref[:, sl].astype(BF16))
    vcat[BAND_ROWS:BAND_ROWS + rows, :] = vo_ref[...].astype(BF16)

    col = lax.broadcasted_iota(jnp.int32, (grows, gw), 1)
    for h in range(HEADS):
        sl = slice(h * HEAD_DIM, (h + 1) * HEAD_DIM)
        bias = bias_scr[h][:, :gw]
        scores = []
        for g in range(nq // gq):
            r0 = g * grows
            q = (q_ref[r0:r0 + grows, sl] * (SCALE * LOG2E)).astype(BF16)
            scores.append(_dot(q, kt[h, :, r0:r0 + gw]))
        probs = []
        for g in range(nq // gq):
            s = scores[g] + bias
            if not has_cache:
                first_valid = jnp.where(t == 0, BAND_ROWS - g * grows, 0)
                s = jnp.where(col < first_valid, NEG_INF, s)
            m = jnp.max(s, axis=-1, keepdims=True)
            p = jnp.exp2(s - m)
            probs.append((p.astype(BF16), jnp.sum(p, axis=-1, keepdims=True)))
        outs = []
        for g in range(nq // gq):
            r0 = g * grows
            outs.append(_dot(probs[g][0], vcat[r0:r0 + gw, sl]))
        for g in range(nq // gq):
            r0 = g * grows
            o = outs[g] / probs[g][1]
            o_ref[r0:r0 + grows, sl] = (o * _silu(z_ref[r0:r0 + grows, sl])).astype(BF16)

    @pl.when(t == nt - 1)
    def _():
        if rows < BAND_ROWS:
            nk_ref[0:BAND_ROWS - rows] = kc_ref[rows:BAND_ROWS]
            nv_ref[0:BAND_ROWS - rows] = vc_ref[rows:BAND_ROWS]
        for h in range(HEADS):
            sl = slice(h * HEAD_DIM, (h + 1) * HEAD_DIM)
            nk_ref[BAND_ROWS - rows:BAND_ROWS, h, :] = ko_ref[:, sl]
            nv_ref[BAND_ROWS - rows:BAND_ROWS, h, :] = vo_ref[:, sl]


def _attn(p3, rev_bias, cache, layer, depth, prev):
    b, t, _ = p3.shape
    has_cache = cache is not None
    rows = CHUNK if has_cache else BAND_ROWS
    assert t % rows == 0 and (not has_cache or t == CHUNK)
    nq = rows // CHUNK
    gq = min(nq, ATTN_GROUP_CHUNKS)
    assert BAND_ROWS + gq * CHUNK <= BIAS_LANES
    cur = lambda blk: pl.BlockSpec((None, rows, WIDTH), lambda i, tt: (i, tt, blk))
    slab = pl.BlockSpec((None, None, BAND_ROWS, HEADS, HEAD_DIM),
                        lambda i, tt: (layer, i, 0, 0, 0))
    in_specs = [cur(0), cur(1), cur(2), cur(3),
                pl.BlockSpec((HEADS, BIAS_LANES), lambda i, tt: (0, 0))]
    args = [p3, p3, p3, p3, rev_bias]
    if has_cache:
        assert cache[0].shape[2:] == (BAND_ROWS, HEADS, HEAD_DIM)
        in_specs += [slab, slab]
        args += list(cache)
    aliases = _stacked_out(prev, (1, 2), in_specs, args)
    return pl.pallas_call(
        functools.partial(_attn_kernel, nq=nq, gq=gq, has_cache=has_cache, n_alias=len(aliases)),
        grid=(b, t // rows),
        in_specs=in_specs,
        out_specs=[cur(0), slab, slab],
        out_shape=[jax.ShapeDtypeStruct((b, t, WIDTH), BF16),
                   jax.ShapeDtypeStruct((depth, b, BAND_ROWS, HEADS, HEAD_DIM), F32),
                   jax.ShapeDtypeStruct((depth, b, BAND_ROWS, HEADS, HEAD_DIM), F32)],
        input_output_aliases=aliases,
        scratch_shapes=[pltpu.VMEM((HEADS, HEAD_DIM, BAND_ROWS + rows), BF16),
                        pltpu.VMEM((BAND_ROWS + rows, WIDTH), BF16),
                        pltpu.VMEM((HEADS, gq * CHUNK, BIAS_LANES), F32)],
        compiler_params=pltpu.CompilerParams(
            dimension_semantics=("arbitrary", "arbitrary"), vmem_limit_bytes=V7X_VMEM_LIMIT),
        name="attn",
    )(*args)


def _delta_kernel(*refs, has_state, nch, n_alias):
    refs = _drop_aliased_inputs(refs, 11 if has_state else 9, n_alias)
    if has_state:
        (q_ref, k_ref, v_ref, z_ref, gt_ref, cw_ref, al_ref, dt_ref, dn_ref, c0_ref, s0_ref,
         o_ref, sout_ref, tail_ref, xbuf, st) = refs
    else:
        (q_ref, k_ref, v_ref, z_ref, gt_ref, cw_ref, al_ref, dt_ref, dn_ref,
         o_ref, sout_ref, tail_ref, xbuf, st) = refs
    c = pl.program_id(1)
    nc = pl.num_programs(1)
    rows = nch * CHUNK

    lane_tiles = [(p, j, slice(p * WIDTH + j * HEAD_DIM, p * WIDTH + (j + 1) * HEAD_DIM))
                  for p in range(3) for j in range(HEADS)]

    @pl.when(c == 0)
    def _():
        if has_state:
            for p, j, cols in lane_tiles:
                xbuf[p, j, 0:8, :] = c0_ref[:, cols]
            st[...] = s0_ref[...]
        else:
            xbuf[:, :, 0:8, :] = jnp.zeros((3, HEADS, 8, HEAD_DIM), F32)
            st[...] = jnp.zeros_like(st)

    @pl.when(c > 0)
    def _():
        xbuf[:, :, 0:8, :] = xbuf[:, :, rows:rows + 8, :]

    for p, src in enumerate((q_ref, k_ref, v_ref)):
        for j in range(HEADS):
            xbuf[p, j, 8:8 + rows, :] = src[:, j * HEAD_DIM:(j + 1) * HEAD_DIM]

    def conv_act(p, h, cc):
        cols = slice(p * WIDTH + h * HEAD_DIM, p * WIDTH + (h + 1) * HEAD_DIM)
        y = None
        for i in range(CONV_W):
            start = 8 - (CONV_W - 1) + i + cc * CHUNK
            term = xbuf[p, h, start:start + CHUNK, :] * cw_ref[i:i + 1, cols]
            y = term if y is None else y + term
        return _silu(y)

    ri = lax.broadcasted_iota(jnp.int32, (CHUNK, CHUNK), 0)
    ci = lax.broadcasted_iota(jnp.int32, (CHUNK, CHUNK), 1)
    causal = ri >= ci
    strict = ri > ci
    ltri = causal.astype(F32)
    dn = dn_ref[...]

    chains = []
    for cc in range(nch):
        rs = slice(cc * CHUNK, (cc + 1) * CHUNK)
        gt = gt_ref[rs, :]
        beta_all = jax.nn.sigmoid(gt)
        g_all = -jnp.exp(al_ref[...]) * _softplus(gt + dt_ref[...])
        gcum = jnp.dot(ltri, g_all, precision=lax.Precision.HIGHEST,
                       preferred_element_type=F32)
        gcum_t = jnp.transpose(gcum)
        eg_all = jnp.exp(gcum)
        glast = gcum[CHUNK - 1:CHUNK, :]
        tail_all = jnp.exp(glast - gcum)
        gblk_all = jnp.exp(glast)
        for h in range(HEADS):
            sl = slice(h * HEAD_DIM, (h + 1) * HEAD_DIM)
            qh, kh, vh = conv_act(0, h, cc), conv_act(1, h, cc), conv_act(2, h, cc)
            qh = qh * lax.rsqrt(jnp.sum(qh * qh, axis=-1, keepdims=True) + EPS)
            kh = kh * lax.rsqrt(jnp.sum(kh * kh, axis=-1, keepdims=True) + EPS)
            beta = beta_all[:, h:h + 1]
            gl = HEADS + h
            eg = eg_all[:, gl:gl + 1]
            diff = gcum[:, gl:gl + 1] - gcum_t[gl:gl + 1, :]
            kbeta = kh * beta
            qs = qh * SCALE
            chains.append(dict(
                rs=rs, sl=sl, h=h,
                decay=jnp.exp(jnp.where(causal, diff, -jnp.inf)),
                k16=kh.astype(BF16), kbeta16=kbeta.astype(BF16), qs16=qs.astype(BF16),
                qdec=qs * eg,
                ktail16=(kh * tail_all[:, gl:gl + 1]).astype(BF16),
                gblk=gblk_all[:, gl:gl + 1],
                x=jnp.concatenate([vh * beta, kbeta * eg], axis=1)))

    for ch in chains:
        ch["a"] = jnp.where(strict, _dot_nt(ch["kbeta16"], ch["k16"]) * ch["decay"], 0.0)
    for ch in chains:
        ch["intra16"] = (_dot_nt(ch["qs16"], ch["k16"]) * ch["decay"]).astype(BF16)

    def same_block(size):
        shift = size.bit_length() - 1
        return jnp.right_shift(ri, shift) == jnp.right_shift(ci, shift)

    eye = (ri == ci).astype(F32)
    for ch in chains:
        n1 = jnp.where(same_block(8), -ch["a"], 0.0)
        ch["n1_16"] = n1.astype(BF16)
        ch["p"] = eye + n1
    for ch in chains:
        ch["n2"] = _dot(ch["n1_16"], ch["n1_16"])
    for ch in chains:
        ch["n2_16"] = ch["n2"].astype(BF16)
        ch["n4"] = _dot(ch["n2_16"], ch["n2_16"])
        ch["p"] = ch["p"] + _dot(ch["p"].astype(BF16), ch["n2_16"])
    for ch in chains:
        ch["t"] = ch["p"] + _dot(ch["p"].astype(BF16), ch["n4"].astype(BF16))
    for size in (8, 16, 32):
        couple = jnp.logical_and(same_block(2 * size), jnp.logical_not(same_block(size)))
        for ch in chains:
            ch["t16"] = ch["t"].astype(BF16)
            ch["te"] = _dot(ch["t16"], jnp.where(couple, ch["a"], 0.0).astype(BF16))
        for ch in chains:
            ch["t"] = ch["t"] - _dot(ch["te"].astype(BF16), ch["t16"])
    for ch in chains:
        ch["x"] = _dot(ch["t"].astype(BF16), ch["x"].astype(BF16))

    for ch in chains:
        x16 = ch["x"].astype(BF16)
        ix = _dot(ch["intra16"], x16)
        kx = _dot_tn(ch["ktail16"], x16)
        ch["o_intra"] = ix[:, :HEAD_DIM]
        ch["qeff16"] = (ch["qdec"] - ix[:, HEAD_DIM:]).astype(BF16)
        ch["b"] = kx[:, :HEAD_DIM]
        ch["p16"] = kx[:, HEAD_DIM:].astype(BF16)

    state = [st[h] for h in range(HEADS)]
    for cc in range(nch):
        for ch in chains[cc * HEADS:(cc + 1) * HEADS]:
            h, rs, sl = ch["h"], ch["rs"], ch["sl"]
            s16 = state[h].astype(BF16)
            o = _dot(ch["qeff16"], s16) + ch["o_intra"]
            state[h] = state[h] * ch["gblk"] - _dot(ch["p16"], s16) + ch["b"]
            o = o * lax.rsqrt(jnp.mean(o * o, axis=-1, keepdims=True) + EPS) * dn
            o_ref[rs, sl] = (o * _silu(z_ref[rs, sl])).astype(BF16)
    for h in range(HEADS):
        st[h] = state[h]

    @pl.when(c == nc - 1)
    def _():
        sout_ref[...] = st[...]
        for p, j, cols in lane_tiles:
            tail_ref[:, cols] = xbuf[p, j, rows:rows + 8, :]


def _delta(p3, gt3, conv_w, al_row, dt_row, dn_row, conv0, s0, layer, depth, prev):
    b, t, _ = p3.shape
    has_state = s0 is not None
    nch = min(MIXER_CHUNKS_PER_STEP, t // CHUNK)
    rows = nch * CHUNK
    assert t % rows == 0
    col = lambda blk: pl.BlockSpec((None, rows, WIDTH), lambda i, c: (i, c, blk))
    full = lambda shape: pl.BlockSpec(shape, lambda i, c: (0,) * len(shape))
    in_specs = [col(4), col(5), col(6), col(7),
                pl.BlockSpec((None, rows, GATE_LANES), lambda i, c: (i, c, 0)),
                full((CONV_W, 3 * WIDTH)), full((1, GATE_LANES)), full((1, GATE_LANES)),
                full((1, HEAD_DIM))]
    args = [p3, p3, p3, p3, gt3, conv_w, al_row, dt_row, dn_row]
    if has_state:
        in_specs += [pl.BlockSpec((None, 8, 3 * WIDTH), lambda i, c: (i, 0, 0)),
                     pl.BlockSpec((None, HEADS, HEAD_DIM, HEAD_DIM), lambda i, c: (i, 0, 0, 0))]
        args += [conv0, s0]
    aliases = _stacked_out(prev, (1, 2), in_specs, args)
    return pl.pallas_call(
        functools.partial(_delta_kernel, has_state=has_state, nch=nch, n_alias=len(aliases)),
        grid=(b, t // rows),
        in_specs=in_specs,
        out_specs=[pl.BlockSpec((None, rows, WIDTH), lambda i, c: (i, c, 0)),
                   pl.BlockSpec((None, None, HEADS, HEAD_DIM, HEAD_DIM),
                                lambda i, c: (layer, i, 0, 0, 0)),
                   pl.BlockSpec((None, None, 8, 3 * WIDTH), lambda i, c: (layer, i, 0, 0))],
        out_shape=[jax.ShapeDtypeStruct((b, t, WIDTH), BF16),
                   jax.ShapeDtypeStruct((depth, b, HEADS, HEAD_DIM, HEAD_DIM), F32),
                   jax.ShapeDtypeStruct((depth, b, 8, 3 * WIDTH), F32)],
        input_output_aliases=aliases,
        scratch_shapes=[pltpu.VMEM((3, HEADS, rows + 8, HEAD_DIM), F32),
                        pltpu.VMEM((HEADS, HEAD_DIM, HEAD_DIM), F32)],
        compiler_params=pltpu.CompilerParams(
            dimension_semantics=("arbitrary", "arbitrary"), vmem_limit_bytes=V7X_VMEM_LIMIT),
        name="delta",
    )(*args)


def _ret_kernel(*refs, has_state, nch, n_alias):
    refs = _drop_aliased_inputs(refs, 12 if has_state else 11, n_alias)
    if has_state:
        (q_ref, k_ref, v_ref, z_ref, cc_ref, ss_ref, dm_ref, xi_ref, zeta_ref, gb_ref, rn_ref,
         s0_ref, o_ref, sout_ref, st) = refs
    else:
        (q_ref, k_ref, v_ref, z_ref, cc_ref, ss_ref, dm_ref, xi_ref, zeta_ref, gb_ref, rn_ref,
         o_ref, sout_ref, st) = refs
    c = pl.program_id(1)
    nc = pl.num_programs(1)

    @pl.when(c == 0)
    def _():
        if has_state:
            st[...] = s0_ref[...]
        else:
            st[...] = jnp.zeros_like(st)

    chains = []
    for cc in range(nch):
        rs = slice(cc * CHUNK, (cc + 1) * CHUNK)
        cos2 = cc_ref[rs, :]
        sin2 = ss_ref[rs, :]
        for h in range(HEADS):
            sl = slice(h * HEAD_DIM, (h + 1) * HEAD_DIM)
            q = q_ref[rs, sl]
            k = k_ref[rs, sl]
            qh = q * cos2 + pltpu.roll(q, HEAD_DIM // 2, axis=1) * sin2
            kh = (k * cos2 + pltpu.roll(k, HEAD_DIM // 2, axis=1) * sin2) * SCALE
            chains.append(dict(rs=rs, sl=sl, h=h, q16=qh.astype(BF16), k16=kh.astype(BF16),
                               kz16=(kh * zeta_ref[h]).astype(BF16),
                               v16=v_ref[rs, sl].astype(BF16)))
    for ch in chains:
        ch["intra16"] = (_dot_nt(ch["q16"], ch["k16"]) * dm_ref[ch["h"]]).astype(BF16)
    for ch in chains:
        ch["kv"] = _dot_tn(ch["kz16"], ch["v16"])
    for ch in chains:
        ch["o_intra"] = _dot(ch["intra16"], ch["v16"])

    state = [st[h] for h in range(HEADS)]
    for cc in range(nch):
        for ch in chains[cc * HEADS:(cc + 1) * HEADS]:
            h, rs, sl = ch["h"], ch["rs"], ch["sl"]
            ch["o"] = ch["o_intra"] + _dot(ch["q16"], state[h].astype(BF16)) * xi_ref[h]
            state[h] = state[h] * gb_ref[h] + ch["kv"]
    for h in range(HEADS):
        st[h] = state[h]

    for ch in chains:
        ch["d"] = ch["o"] - jnp.mean(ch["o"], axis=-1, keepdims=True)
    for ch in chains:
        ch["var"] = jnp.mean(ch["d"] * ch["d"], axis=-1, keepdims=True)
    for ch in chains:
        rs, sl = ch["rs"], ch["sl"]
        o = ch["d"] * lax.rsqrt(ch["var"] + EPS) * rn_ref[:, sl]
        o_ref[rs, sl] = (o * _silu(z_ref[rs, sl])).astype(BF16)

    @pl.when(c == nc - 1)
    def _():
        sout_ref[...] = st[...]


def _ret(p3, cc, ss, consts, rn_row, s0, layer, depth, prev):
    b, t, _ = p3.shape
    has_state = s0 is not None
    nch = min(RET_CHUNKS_PER_STEP, t // CHUNK)
    rows = nch * CHUNK
    assert t % rows == 0
    dmat, xi, zeta, gblk = consts
    col = lambda blk: pl.BlockSpec((None, rows, WIDTH), lambda i, c: (i, c, blk))
    full = lambda shape: pl.BlockSpec(shape, lambda i, c: (0,) * len(shape))
    tab = pl.BlockSpec((rows, HEAD_DIM), lambda i, c: (c, 0))
    in_specs = [col(8), col(9), col(10), col(11), tab, tab,
                full((HEADS, CHUNK, CHUNK)), full((HEADS, CHUNK, HEAD_DIM)),
                full((HEADS, CHUNK, HEAD_DIM)), full((HEADS, 1, HEAD_DIM)), full((1, WIDTH))]
    args = [p3, p3, p3, p3, cc, ss, dmat, xi, zeta, gblk, rn_row]
    if has_state:
        in_specs.append(pl.BlockSpec((None, HEADS, HEAD_DIM, HEAD_DIM), lambda i, c: (i, 0, 0, 0)))
        args.append(s0)
    aliases = _stacked_out(prev, (1,), in_specs, args)
    return pl.pallas_call(
        functools.partial(_ret_kernel, has_state=has_state, nch=nch, n_alias=len(aliases)),
        grid=(b, t // rows),
        in_specs=in_specs,
        out_specs=[pl.BlockSpec((None, rows, WIDTH), lambda i, c: (i, c, 0)),
                   pl.BlockSpec((None, None, HEADS, HEAD_DIM, HEAD_DIM),
                                lambda i, c: (layer, i, 0, 0, 0))],
        out_shape=[jax.ShapeDtypeStruct((b, t, WIDTH), BF16),
                   jax.ShapeDtypeStruct((depth, b, HEADS, HEAD_DIM, HEAD_DIM), F32)],
        input_output_aliases=aliases,
        scratch_shapes=[pltpu.VMEM((HEADS, HEAD_DIM, HEAD_DIM), F32)],
        compiler_params=pltpu.CompilerParams(
            dimension_semantics=("arbitrary", "arbitrary"), vmem_limit_bytes=V7X_VMEM_LIMIT),
        name="ret",
    )(*args)


def _merge_kernel(oa_ref, ob_ref, oc_ref, ga_ref, gb_ref, gc_ref, wa_ref, wb_ref, wc_ref,
                  wo_ref, x_ref, g_ref, o_ref):
    merged = (jax.nn.sigmoid(ga_ref[...]) * _dot(oa_ref[...], wa_ref[...])
              + jax.nn.sigmoid(gb_ref[...]) * _dot(ob_ref[...], wb_ref[...])
              + jax.nn.sigmoid(gc_ref[...]) * _dot(oc_ref[...], wc_ref[...]))
    y = _dot(merged.astype(BF16), wo_ref[...])
    y = y * lax.rsqrt(jnp.mean(y * y, axis=-1, keepdims=True) + EPS) * g_ref[...]
    o_ref[...] = x_ref[...] + y


def _merge(oa, ob, oc, p2, x2d, wa, wb, wc, wo, g):
    m = x2d.shape[0]
    tm = min(m, MERGE_ROWS)
    assert m % tm == 0
    row = lambda w: pl.BlockSpec((tm, w), lambda i: (i, 0))
    gate = lambda blk: pl.BlockSpec((tm, D_MODEL), lambda i: (i, blk))
    const = lambda shape: pl.BlockSpec(shape, lambda i: (0, 0), pipeline_mode=pl.Buffered(1))
    return pl.pallas_call(
        _merge_kernel,
        grid=(m // tm,),
        in_specs=[row(WIDTH), row(WIDTH), row(WIDTH), gate(6), gate(7), gate(8),
                  const((WIDTH, D_MODEL)), const((WIDTH, D_MODEL)), const((WIDTH, D_MODEL)),
                  const((D_MODEL, D_MODEL)), row(D_MODEL), const((1, D_MODEL))],
        out_specs=row(D_MODEL),
        out_shape=jax.ShapeDtypeStruct((m, D_MODEL), F32),
        compiler_params=pltpu.CompilerParams(
            dimension_semantics=("arbitrary",), vmem_limit_bytes=V7X_VMEM_LIMIT),
        name="merge",
    )(oa, ob, oc, p2, p2, p2, wa, wb, wc, wo, x2d, g)


def _retention_constants():
    log_gamma = jnp.log1p(-jnp.exp2(-5.0 - jnp.arange(HEADS, dtype=F32)))
    idx = jnp.arange(CHUNK, dtype=F32)
    rel = idx[:, None] - idx[None, :]
    lg = log_gamma[:, None, None]
    dmat = jnp.where(rel >= 0, jnp.exp(lg * jnp.maximum(rel, 0.0)), 0.0)
    xi = jnp.exp(log_gamma[:, None] * (idx + 1.0))
    zeta = jnp.exp(log_gamma[:, None] * (CHUNK - 1.0 - idx))
    gblk = jnp.exp(log_gamma * CHUNK)
    bc = lambda v: jnp.broadcast_to(v[:, :, None], (HEADS, v.shape[1], HEAD_DIM))
    return dmat, bc(xi), bc(zeta), bc(gblk[:, None])


def _rotary_tables(pos):
    half = HEAD_DIM // 2
    inv = ROPE_BASE ** (-jnp.arange(half, dtype=F32) / half)
    ang = pos.astype(F32)[:, None] * inv[None, :]
    cos, sin = jnp.cos(ang), jnp.sin(ang)
    return jnp.concatenate([cos, cos], axis=1), jnp.concatenate([-sin, sin], axis=1)


def _rev_bias(rel_bias):
    n_off = BAND_KEYS + CHUNK - 1
    head = rel_bias[:, MAX_REL - (CHUNK - 1):].astype(F32)
    edge = jnp.broadcast_to(head[:, -1:], (rel_bias.shape[0], n_off - head.shape[1]))
    ext = jnp.concatenate([head, edge], axis=1)
    return jnp.pad(ext[:, ::-1], ((0, 0), (0, BIAS_LANES - n_off)))


def _lane_row(v, lane0):
    return jnp.zeros((1, GATE_LANES), F32).at[0, lane0:lane0 + v.shape[0]].set(v.astype(F32))


def _group_layer(x, pos, cache, conv_buf, s_delta, s_ret, lw, ret_consts, layer, depth, prev):
    (norm_pre, norm_post, w_main, w_gate, bias, conv_w, al_row, dt_row, dn_row, rn_row,
     wa, wb, wc, wo) = lw
    b, t, _ = x.shape
    x2d = x.reshape(b * t, D_MODEL)
    p2, gt2 = _inproj(x2d, norm_pre, w_main, w_gate)
    p3 = p2.reshape(b, t, N_MAIN)
    gt3 = gt2.reshape(b, t, GATE_LANES)
    prev_k, prev_v, prev_sd, prev_tail, prev_sr = prev

    oa, new_k, new_v = _attn(p3, bias, cache, layer, depth, (prev_k, prev_v))

    conv0 = None if conv_buf is None else jnp.pad(conv_buf, ((0, 0), (8 - (CONV_W - 1), 0), (0, 0)))
    ob, new_sd, conv_tail = _delta(p3, gt3, conv_w, al_row, dt_row, dn_row, conv0, s_delta,
                                   layer, depth, (prev_sd, prev_tail))

    cc, ss = _rotary_tables(pos)
    oc, new_sr = _ret(p3, cc, ss, ret_consts, rn_row, s_ret, layer, depth, (prev_sr,))

    y2d = _merge(oa.reshape(b * t, WIDTH), ob.reshape(b * t, WIDTH), oc.reshape(b * t, WIDTH),
                 p2, x2d, wa, wb, wc, wo, norm_post)
    return y2d.reshape(b, t, D_MODEL), (new_k, new_v, new_sd, conv_tail, new_sr)


def kernel(x_prompt, x_sample, cache_attn_k, cache_attn_v, state_conv, state_delta, state_ret,
           norm_pre, norm_post, w_in, attn_rel_bias, conv_w, delta_a_log, delta_dt_bias,
           delta_norm, ret_norm, w_branch_a, w_branch_b, w_branch_c, w_out):
    depth = w_in.shape[0]
    tp, ts = x_prompt.shape[1], x_sample.shape[1]
    pos_p = jnp.arange(tp, dtype=jnp.int32)
    pos_s = PAST_LEN + jnp.arange(ts, dtype=jnp.int32)
    ret_consts = _retention_constants()
    w_in_t = jnp.swapaxes(w_in, 1, 2)
    xp, xs = x_prompt, x_sample

    def empty_states(b):
        stacked = lambda *shape: jnp.zeros((depth, b) + shape, F32)
        return (stacked(BAND_ROWS, HEADS, HEAD_DIM), stacked(BAND_ROWS, HEADS, HEAD_DIM),
                stacked(HEADS, HEAD_DIM, HEAD_DIM), stacked(8, 3 * WIDTH),
                stacked(HEADS, HEAD_DIM, HEAD_DIM))

    st_p, st_s = empty_states(x_prompt.shape[0]), empty_states(x_sample.shape[0])
    for l in range(depth):
        w_main, w_gate = _weight_prep(w_in_t, l)
        lw = (norm_pre[l][None, :], norm_post[l][None, :], w_main, w_gate,
              _rev_bias(attn_rel_bias[l]), conv_w[l],
              _lane_row(delta_a_log[l], HEADS), _lane_row(delta_dt_bias[l], HEADS),
              delta_norm[l][None, :].astype(F32), ret_norm[l].reshape(1, WIDTH).astype(F32),
              w_branch_a[l].astype(BF16), w_branch_b[l].astype(BF16), w_branch_c[l].astype(BF16),
              w_out[l].astype(BF16))
        xp, st_p = _group_layer(xp, pos_p, None, None, None, None, lw, ret_consts,
                                l, depth, st_p)
        xs, st_s = _group_layer(xs, pos_s, (cache_attn_k, cache_attn_v), state_conv[l],
                                state_delta[l], state_ret[l], lw, ret_consts, l, depth, st_s)

    def finish(stacked):
        new_k, new_v, new_sd, conv_tail, new_sr = stacked
        return new_k, new_v, conv_tail[:, :, 8 - (CONV_W - 1):, :], new_sd, new_sr

    return (xp, xs, *finish(st_p), *finish(st_s))
```

```python
import functools

import jax
import jax.numpy as jnp
from jax import lax
from jax.experimental import pallas as pl
from jax.experimental.pallas import tpu as pltpu

F32 = jnp.float32
BF16 = jnp.bfloat16

D_MODEL = 2048
CHUNK = 64
HEAD_DIM = 128
HEADS = 8
WIDTH = HEADS * HEAD_DIM
BAND_CHUNKS = 8
BAND_ROWS = BAND_CHUNKS * CHUNK
BAND_KEYS = BAND_ROWS + CHUNK
BIAS_LANES = 640
MAX_REL = 128
CONV_W = 4
PAST_LEN = 2048
ROPE_BASE = 10000.0
EPS = 1e-6
NEG_INF = -1e30
SCALE = HEAD_DIM ** -0.5
LOG2E = 1.4426950408889634

N_MAIN = 18 * WIDTH
GATE_COL0 = 8 * WIDTH
GATE_LANES = 128
V7X_VMEM_LIMIT = 60 * 1024 * 1024
MIXER_CHUNKS_PER_STEP = 4
RET_CHUNKS_PER_STEP = 8
ATTN_GROUP_CHUNKS = 2
INPROJ_ROWS = 1024
INPROJ_COLS = 2048
MERGE_ROWS = 256
assert MAX_REL >= CHUNK - 1

_NT = (((1,), (1,)), ((), ()))
_TN = (((0,), (0,)), ((), ()))


def _dot(a, b):
    return jnp.dot(a, b, preferred_element_type=F32)


def _dot_nt(a, b):
    return lax.dot_general(a, b, _NT, preferred_element_type=F32)


def _dot_tn(a, b):
    return lax.dot_general(a, b, _TN, preferred_element_type=F32)


def _silu(x):
    return x * jax.nn.sigmoid(x)


def _softplus(x):
    return jnp.maximum(x, 0.0) + jnp.log1p(jnp.exp(-jnp.abs(x)))


PREP_COLS = 512
N_GATE = 2 * HEADS
assert GATE_COL0 % PREP_COLS == 0 and N_GATE % 8 == 0 and N_GATE <= GATE_LANES


def _weight_prep_kernel(a_ref, b_ref, o_ref, og_ref):
    n = pl.program_id(0)
    first_shifted = GATE_COL0 // PREP_COLS

    @pl.when(n < first_shifted)
    def _():
        o_ref[...] = jnp.transpose(a_ref[...]).astype(BF16)

    @pl.when(n >= first_shifted)
    def _():
        src = jnp.concatenate([a_ref[N_GATE:PREP_COLS, :], b_ref[...]], axis=0)
        o_ref[...] = jnp.transpose(src).astype(BF16)

    @pl.when(n == first_shifted)
    def _():
        pad = jnp.zeros((GATE_LANES - N_GATE, D_MODEL), F32)
        og_ref[...] = jnp.concatenate([a_ref[0:N_GATE, :], pad], axis=0).astype(BF16)


def _weight_prep(w_in_t, layer):
    assert w_in_t.shape[1] == N_MAIN + N_GATE
    return pl.pallas_call(
        _weight_prep_kernel,
        grid=(N_MAIN // PREP_COLS,),
        in_specs=[pl.BlockSpec((None, PREP_COLS, D_MODEL), lambda n: (layer, n, 0)),
                  pl.BlockSpec((None, N_GATE, D_MODEL),
                               lambda n: (layer, (n + 1) * (PREP_COLS // N_GATE), 0))],
        out_specs=[pl.BlockSpec((D_MODEL, PREP_COLS), lambda n: (0, n)),
                   pl.BlockSpec((GATE_LANES, D_MODEL), lambda n: (0, 0))],
        out_shape=[jax.ShapeDtypeStruct((D_MODEL, N_MAIN), BF16),
                   jax.ShapeDtypeStruct((GATE_LANES, D_MODEL), BF16)],
        compiler_params=pltpu.CompilerParams(
            dimension_semantics=("arbitrary",), vmem_limit_bytes=V7X_VMEM_LIMIT),
        name="wprep",
    )(w_in_t, w_in_t)


def _inproj_kernel(x_ref, g_ref, w_ref, wg_ref, o_ref, og_ref, h_ref):
    @pl.when(pl.program_id(1) == 0)
    def _():
        x = x_ref[...]
        ms = jnp.mean(x * x, axis=-1, keepdims=True)
        h = (x * lax.rsqrt(ms + EPS) * g_ref[...]).astype(BF16)
        h_ref[...] = h
        og_ref[...] = _dot_nt(h, wg_ref[...])

    o_ref[...] = _dot(h_ref[...], w_ref[...])


def _inproj(x2d, g, w_main, w_gate):
    m = x2d.shape[0]
    tm = min(m, INPROJ_ROWS)
    tn = INPROJ_COLS
    assert m % tm == 0 and N_MAIN % tn == 0
    return pl.pallas_call(
        _inproj_kernel,
        grid=(m // tm, N_MAIN // tn),
        in_specs=[
            pl.BlockSpec((tm, D_MODEL), lambda i, n: (i, 0)),
            pl.BlockSpec((1, D_MODEL), lambda i, n: (0, 0)),
            pl.BlockSpec((D_MODEL, tn), lambda i, n: (0, n)),
            pl.BlockSpec((GATE_LANES, D_MODEL), lambda i, n: (0, 0)),
        ],
        out_specs=[
            pl.BlockSpec((tm, tn), lambda i, n: (i, n)),
            pl.BlockSpec((tm, GATE_LANES), lambda i, n: (i, 0)),
        ],
        out_shape=[
            jax.ShapeDtypeStruct((m, N_MAIN), F32),
            jax.ShapeDtypeStruct((m, GATE_LANES), F32),
        ],
        scratch_shapes=[pltpu.VMEM((tm, D_MODEL), BF16)],
        compiler_params=pltpu.CompilerParams(
            dimension_semantics=("arbitrary", "arbitrary"),
            vmem_limit_bytes=V7X_VMEM_LIMIT),
        name="inproj",
    )(x2d, g, w_main, w_gate)


def _drop_aliased_inputs(refs, n_in, n_alias):
    return refs[:n_in] + refs[n_in + n_alias:]


def _stacked_out(prev, out_positions, in_specs, args):
    aliases = {len(args) + k: pos for k, pos in enumerate(out_positions)}
    in_specs += [pl.BlockSpec(memory_space=pl.ANY)] * len(prev)
    args += list(prev)
    return aliases


def _attn_kernel(*refs, nq, gq, has_cache, n_alias):
    refs = _drop_aliased_inputs(refs, 7 if has_cache else 5, n_alias)
    if has_cache:
        (q_ref, ko_ref, vo_ref, z_ref, rb_ref, kc_ref, vc_ref,
         o_ref, nk_ref, nv_ref, kt, vcat, bias_scr) = refs
    else:
        (q_ref, ko_ref, vo_ref, z_ref, rb_ref,
         o_ref, nk_ref, nv_ref, kt, vcat, bias_scr) = refs
    t = pl.program_id(1)
    nt = pl.num_programs(1)
    rows = nq * CHUNK
    grows = gq * CHUNK
    gw = BAND_ROWS + grows

    @pl.when(jnp.logical_and(pl.program_id(0) == 0, t == 0))
    def _():
        lane = lax.broadcasted_iota(jnp.int32, (CHUNK, BIAS_LANES), 1)
        for h in range(HEADS):
            wide = jnp.broadcast_to(rb_ref[h:h + 1, :], (CHUNK, BIAS_LANES))
            for e in range(gq):
                shift = (BIAS_LANES - (CHUNK - 1) + e * CHUNK) % BIAS_LANES
                tab = pltpu.roll(wide, shift, 1, stride=1, stride_axis=0) * LOG2E
                tab = jnp.where(lane < e * CHUNK, NEG_INF, tab)
                tab = jnp.where(lane >= e * CHUNK + BAND_KEYS, NEG_INF, tab)
                bias_scr[h, e * CHUNK:(e + 1) * CHUNK, :] = tab

    eye16 = (lax.broadcasted_iota(jnp.int32, (HEAD_DIM, HEAD_DIM), 0)
             == lax.broadcasted_iota(jnp.int32, (HEAD_DIM, HEAD_DIM), 1)).astype(BF16)

    def transpose16(x16):
        return _dot_nt(eye16, x16).astype(BF16)

    if has_cache:
        for h in range(HEADS):
            sl = slice(h * HEAD_DIM, (h + 1) * HEAD_DIM)
            kt[h, :, 0:BAND_ROWS] = transpose16(kc_ref[:, h, :].astype(BF16))
            vcat[0:BAND_ROWS, sl] = vc_ref[:, h, :].astype(BF16)
    else:
        @pl.when(t == 0)
        def _():
            kt[:, :, 0:BAND_ROWS] = jnp.zeros((HEADS, HEAD_DIM, BAND_ROWS), BF16)
            vcat[0:BAND_ROWS, :] = jnp.zeros((BAND_ROWS, WIDTH), BF16)

        @pl.when(t > 0)
        def _():
            kt[:, :, 0:BAND_ROWS] = kt[:, :, rows:rows + BAND_ROWS]
            vcat[0:BAND_ROWS, :] = vcat[rows:rows + BAND_ROWS, :]
    for h in range(HEADS):
        sl = slice(h * HEAD_DIM, (h + 1) * HEAD_DIM)
        kt[h, :, BAND_ROWS:BAND_ROWS + rows] = transpose16(ko_ref[:, sl].astype(BF16))
    vcat[BAND_ROWS:BAND_ROWS + rows, :] = vo_ref[...].astype(BF16)

    col = lax.broadcasted_iota(jnp.int32, (grows, gw), 1)
    for h in range(HEADS):
        sl = slice(h * HEAD_DIM, (h + 1) * HEAD_DIM)
        bias = bias_scr[h][:, :gw]
        scores = []
        for g in range(nq // gq):
            r0 = g * grows
            q = (q_ref[r0:r0 + grows, sl] * (SCALE * LOG2E)).astype(BF16)
            scores.append(_dot(q, kt[h, :, r0:r0 + gw]))
        probs = []
        for g in range(nq // gq):
            s = scores[g] + bias
            if not has_cache:
                first_valid = jnp.where(t == 0, BAND_ROWS - g * grows, 0)
                s = jnp.where(col < first_valid, NEG_INF, s)
            m = jnp.max(s, axis=-1, keepdims=True)
            p = jnp.exp2(s - m)
            probs.append((p.astype(BF16), jnp.sum(p, axis=-1, keepdims=True)))
        outs = []
        for g in range(nq // gq):
            r0 = g * grows
            outs.append(_dot(probs[g][0], vcat[r0:r0 + gw, sl]))
        for g in range(nq // gq):
            r0 = g * grows
            o = outs[g] / probs[g][1]
            o_ref[r0:r0 + grows, sl] = (o * _silu(z_ref[r0:r0 + grows, sl])).astype(BF16)

    @pl.when(t == nt - 1)
    def _():
        if rows < BAND_ROWS:
            nk_ref[0:BAND_ROWS - rows] = kc_ref[rows:BAND_ROWS]
            nv_ref[0:BAND_ROWS - rows] = vc_ref[rows:BAND_ROWS]
        for h in range(HEADS):
            sl = slice(h * HEAD_DIM, (h + 1) * HEAD_DIM)
            nk_ref[BAND_ROWS - rows:BAND_ROWS, h, :] = ko_ref[:, sl]
            nv_ref[BAND_ROWS - rows:BAND_ROWS, h, :] = vo_ref[:, sl]


def _attn(p3, rev_bias, cache, layer, depth, prev):
    b, t, _ = p3.shape
    has_cache = cache is not None
    rows = CHUNK if has_cache else BAND_ROWS
    assert t % rows == 0 and (not has_cache or t == CHUNK)
    nq = rows // CHUNK
    gq = min(nq, ATTN_GROUP_CHUNKS)
    assert BAND_ROWS + gq * CHUNK <= BIAS_LANES
    cur = lambda blk: pl.BlockSpec((None, rows, WIDTH), lambda i, tt: (i, tt, blk))
    slab = pl.BlockSpec((None, None, BAND_ROWS, HEADS, HEAD_DIM),
                        lambda i, tt: (layer, i, 0, 0, 0))
    in_specs = [cur(0), cur(1), cur(2), cur(3),
                pl.BlockSpec((HEADS, BIAS_LANES), lambda i, tt: (0, 0))]
    args = [p3, p3, p3, p3, rev_bias]
    if has_cache:
        assert cache[0].shape[2:] == (BAND_ROWS, HEADS, HEAD_DIM)
        in_specs += [slab, slab]
        args += list(cache)
    aliases = _stacked_out(prev, (1, 2), in_specs, args)
    return pl.pallas_call(
        functools.partial(_attn_kernel, nq=nq, gq=gq, has_cache=has_cache, n_alias=len(aliases)),
        grid=(b, t // rows),
        in_specs=in_specs,
        out_specs=[cur(0), slab, slab],
        out_shape=[jax.ShapeDtypeStruct((b, t, WIDTH), BF16),
                   jax.ShapeDtypeStruct((depth, b, BAND_ROWS, HEADS, HEAD_DIM), F32),
                   jax.ShapeDtypeStruct((depth, b, BAND_ROWS, HEADS, HEAD_DIM), F32)],
        input_output_aliases=aliases,
        scratch_shapes=[pltpu.VMEM((HEADS, HEAD_DIM, BAND_ROWS + rows), BF16),
                        pltpu.VMEM((BAND_ROWS + rows, WIDTH), BF16),
                        pltpu.VMEM((HEADS, gq * CHUNK, BIAS_LANES), F32)],
        compiler_params=pltpu.CompilerParams(
            dimension_semantics=("arbitrary", "arbitrary"), vmem_limit_bytes=V7X_VMEM_LIMIT),
        name="attn",
    )(*args)


def _delta_kernel(*refs, has_state, nch, n_alias):
    refs = _drop_aliased_inputs(refs, 11 if has_state else 9, n_alias)
    if has_state:
        (q_ref, k_ref, v_ref, z_ref, gt_ref, cw_ref, al_ref, dt_ref, dn_ref, c0_ref, s0_ref,
         o_ref, sout_ref, tail_ref, xbuf, st) = refs
    else:
        (q_ref, k_ref, v_ref, z_ref, gt_ref, cw_ref, al_ref, dt_ref, dn_ref,
         o_ref, sout_ref, tail_ref, xbuf, st) = refs
    c = pl.program_id(1)
    nc = pl.num_programs(1)
    rows = nch * CHUNK

    lane_tiles = [(p, j, slice(p * WIDTH + j * HEAD_DIM, p * WIDTH + (j + 1) * HEAD_DIM))
                  for p in range(3) for j in range(HEADS)]

    @pl.when(c == 0)
    def _():
        if has_state:
            for p, j, cols in lane_tiles:
                xbuf[p, j, 0:8, :] = c0_ref[:, cols]
            st[...] = s0_ref[...]
        else:
            xbuf[:, :, 0:8, :] = jnp.zeros((3, HEADS, 8, HEAD_DIM), F32)
            st[...] = jnp.zeros_like(st)

    @pl.when(c > 0)
    def _():
        xbuf[:, :, 0:8, :] = xbuf[:, :, rows:rows + 8, :]

    for p, src in enumerate((q_ref, k_ref, v_ref)):
        for j in range(HEADS):
            xbuf[p, j, 8:8 + rows, :] = src[:, j * HEAD_DIM:(j + 1) * HEAD_DIM]

    def conv_act(p, h, cc):
        cols = slice(p * WIDTH + h * HEAD_DIM, p * WIDTH + (h + 1) * HEAD_DIM)
        y = None
        for i in range(CONV_W):
            start = 8 - (CONV_W - 1) + i + cc * CHUNK
            term = xbuf[p, h, start:start + CHUNK, :] * cw_ref[i:i + 1, cols]
            y = term if y is None else y + term
        return _silu(y)

    ri = lax.broadcasted_iota(jnp.int32, (CHUNK, CHUNK), 0)
    ci = lax.broadcasted_iota(jnp.int32, (CHUNK, CHUNK), 1)
    causal = ri >= ci
    strict = ri > ci
    ltri = causal.astype(F32)
    dn = dn_ref[...]

    chains = []
    for cc in range(nch):
        rs = slice(cc * CHUNK, (cc + 1) * CHUNK)
        gt = gt_ref[rs, :]
        beta_all = jax.nn.sigmoid(gt)
        g_all = -jnp.exp(al_ref[...]) * _softplus(gt + dt_ref[...])
        gcum = jnp.dot(ltri, g_all, precision=lax.Precision.HIGHEST,
                       preferred_element_type=F32)
        gcum_t = jnp.transpose(gcum)
        eg_all = jnp.exp(gcum)
        glast = gcum[CHUNK - 1:CHUNK, :]
        tail_all = jnp.exp(glast - gcum)
        gblk_all = jnp.exp(glast)
        for h in range(HEADS):
            sl = slice(h * HEAD_DIM, (h + 1) * HEAD_DIM)
            qh, kh, vh = conv_act(0, h, cc), conv_act(1, h, cc), conv_act(2, h, cc)
            qh = qh * lax.rsqrt(jnp.sum(qh * qh, axis=-1, keepdims=True) + EPS)
            kh = kh * lax.rsqrt(jnp.sum(kh * kh, axis=-1, keepdims=True) + EPS)
            beta = beta_all[:, h:h + 1]
            gl = HEADS + h
            eg = eg_all[:, gl:gl + 1]
            diff = gcum[:, gl:gl + 1] - gcum_t[gl:gl + 1, :]
            kbeta = kh * beta
            qs = qh * SCALE
            chains.append(dict(
                rs=rs, sl=sl, h=h,
                decay=jnp.exp(jnp.where(causal, diff, -jnp.inf)),
                k16=kh.astype(BF16), kbeta16=kbeta.astype(BF16), qs16=qs.astype(BF16),
                qdec=qs * eg,
                ktail16=(kh * tail_all[:, gl:gl + 1]).astype(BF16),
                gblk=gblk_all[:, gl:gl + 1],
                x=jnp.concatenate([vh * beta, kbeta * eg], axis=1)))

    for ch in chains:
        ch["a"] = jnp.where(strict, _dot_nt(ch["kbeta16"], ch["k16"]) * ch["decay"], 0.0)
    for ch in chains:
        ch["intra16"] = (_dot_nt(ch["qs16"], ch["k16"]) * ch["decay"]).astype(BF16)

    def same_block(size):
        shift = size.bit_length() - 1
        return jnp.right_shift(ri, shift) == jnp.right_shift(ci, shift)

    eye = (ri == ci).astype(F32)
    for ch in chains:
        n1 = jnp.where(same_block(8), -ch["a"], 0.0)
        ch["n1_16"] = n1.astype(BF16)
        ch["p"] = eye + n1
    for ch in chains:
        ch["n2"] = _dot(ch["n1_16"], ch["n1_16"])
    for ch in chains:
        ch["n2_16"] = ch["n2"].astype(BF16)
        ch["n4"] = _dot(ch["n2_16"], ch["n2_16"])
        ch["p"] = ch["p"] + _dot(ch["p"].astype(BF16), ch["n2_16"])
    for ch in chains:
        ch["t"] = ch["p"] + _dot(ch["p"].astype(BF16), ch["n4"].astype(BF16))
    for size in (8, 16, 32):
        couple = jnp.logical_and(same_block(2 * size), jnp.logical_not(same_block(size)))
        for ch in chains:
            ch["t16"] = ch["t"].astype(BF16)
            ch["te"] = _dot(ch["t16"], jnp.where(couple, ch["a"], 0.0).astype(BF16))
        for ch in chains:
            ch["t"] = ch["t"] - _dot(ch["te"].astype(BF16), ch["t16"])
    for ch in chains:
        ch["x"] = _dot(ch["t"].astype(BF16), ch["x"].astype(BF16))

    for ch in chains:
        x16 = ch["x"].astype(BF16)
        ix = _dot(ch["intra16"], x16)
        kx = _dot_tn(ch["ktail16"], x16)
        ch["o_intra"] = ix[:, :HEAD_DIM]
        ch["qeff16"] = (ch["qdec"] - ix[:, HEAD_DIM:]).astype(BF16)
        ch["b"] = kx[:, :HEAD_DIM]
        ch["p16"] = kx[:, HEAD_DIM:].astype(BF16)

    state = [st[h] for h in range(HEADS)]
    for cc in range(nch):
        for ch in chains[cc * HEADS:(cc + 1) * HEADS]:
            h, rs, sl = ch["h"], ch["rs"], ch["sl"]
            s16 = state[h].astype(BF16)
            o = _dot(ch["qeff16"], s16) + ch["o_intra"]
            state[h] = state[h] * ch["gblk"] - _dot(ch["p16"], s16) + ch["b"]
            o = o * lax.rsqrt(jnp.mean(o * o, axis=-1, keepdims=True) + EPS) * dn
            o_ref[rs, sl] = (o * _silu(z_ref[rs, sl])).astype(BF16)
    for h in range(HEADS):
        st[h] = state[h]

    @pl.when(c == nc - 1)
    def _():
        sout_ref[...] = st[...]
        for p, j, cols in lane_tiles:
            tail_ref[:, cols] = xbuf[p, j, rows:rows + 8, :]


def _delta(p3, gt3, conv_w, al_row, dt_row, dn_row, conv0, s0, layer, depth, prev):
    b, t, _ = p3.shape
    has_state = s0 is not None
    nch = min(MIXER_CHUNKS_PER_STEP, t // CHUNK)
    rows = nch * CHUNK
    assert t % rows == 0
    col = lambda blk: pl.BlockSpec((None, rows, WIDTH), lambda i, c: (i, c, blk))
    full = lambda shape: pl.BlockSpec(shape, lambda i, c: (0,) * len(shape))
    in_specs = [col(4), col(5), col(6), col(7),
                pl.BlockSpec((None, rows, GATE_LANES), lambda i, c: (i, c, 0)),
                full((CONV_W, 3 * WIDTH)), full((1, GATE_LANES)), full((1, GATE_LANES)),
                full((1, HEAD_DIM))]
    args = [p3, p3, p3, p3, gt3, conv_w, al_row, dt_row, dn_row]
    if has_state:
        in_specs += [pl.BlockSpec((None, 8, 3 * WIDTH), lambda i, c: (i, 0, 0)),
                     pl.BlockSpec((None, HEADS, HEAD_DIM, HEAD_DIM), lambda i, c: (i, 0, 0, 0))]
        args += [conv0, s0]
    aliases = _stacked_out(prev, (1, 2), in_specs, args)
    return pl.pallas_call(
        functools.partial(_delta_kernel, has_state=has_state, nch=nch, n_alias=len(aliases)),
        grid=(b, t // rows),
        in_specs=in_specs,
        out_specs=[pl.BlockSpec((None, rows, WIDTH), lambda i, c: (i, c, 0)),
                   pl.BlockSpec((None, None, HEADS, HEAD_DIM, HEAD_DIM),
                                lambda i, c: (layer, i, 0, 0, 0)),
                   pl.BlockSpec((None, None, 8, 3 * WIDTH), lambda i, c: (layer, i, 0, 0))],
        out_shape=[jax.ShapeDtypeStruct((b, t, WIDTH), BF16),
                   jax.ShapeDtypeStruct((depth, b, HEADS, HEAD_DIM, HEAD_DIM), F32),
                   jax.ShapeDtypeStruct((depth, b, 8, 3 * WIDTH), F32)],
        input_output_aliases=aliases,
        scratch_shapes=[pltpu.VMEM((3, HEADS, -(-(rows + 8) // 32) * 32, HEAD_DIM), F32),
                        pltpu.VMEM((HEADS, HEAD_DIM, HEAD_DIM), F32)],
        compiler_params=pltpu.CompilerParams(
            dimension_semantics=("arbitrary", "arbitrary"), vmem_limit_bytes=V7X_VMEM_LIMIT),
        name="delta",
    )(*args)


def _ret_kernel(*refs, has_state, nch, n_alias):
    refs = _drop_aliased_inputs(refs, 12 if has_state else 11, n_alias)
    if has_state:
        (q_ref, k_ref, v_ref, z_ref, cc_ref, ss_ref, dm_ref, xi_ref, zeta_ref, gb_ref, rn_ref,
         s0_ref, o_ref, sout_ref, st) = refs
    else:
        (q_ref, k_ref, v_ref, z_ref, cc_ref, ss_ref, dm_ref, xi_ref, zeta_ref, gb_ref, rn_ref,
         o_ref, sout_ref, st) = refs
    c = pl.program_id(1)
    nc = pl.num_programs(1)

    @pl.when(c == 0)
    def _():
        if has_state:
            st[...] = s0_ref[...]
        else:
            st[...] = jnp.zeros_like(st)

    chains = []
    for cc in range(nch):
        rs = slice(cc * CHUNK, (cc + 1) * CHUNK)
        cos2 = cc_ref[rs, :]
        sin2 = ss_ref[rs, :]
        for h in range(HEADS):
            sl = slice(h * HEAD_DIM, (h + 1) * HEAD_DIM)
            q = q_ref[rs, sl]
            k = k_ref[rs, sl]
            qh = q * cos2 + pltpu.roll(q, HEAD_DIM // 2, axis=1) * sin2
            kh = (k * cos2 + pltpu.roll(k, HEAD_DIM // 2, axis=1) * sin2) * SCALE
            chains.append(dict(rs=rs, sl=sl, h=h, q16=qh.astype(BF16), k16=kh.astype(BF16),
                               kz16=(kh * zeta_ref[h]).astype(BF16),
                               v16=v_ref[rs, sl].astype(BF16)))
    for ch in chains:
        ch["intra16"] = (_dot_nt(ch["q16"], ch["k16"]) * dm_ref[ch["h"]]).astype(BF16)
    for ch in chains:
        ch["kv"] = _dot_tn(ch["kz16"], ch["v16"])
    for ch in chains:
        ch["o_intra"] = _dot(ch["intra16"], ch["v16"])

    state = [st[h] for h in range(HEADS)]
    for cc in range(nch):
        for ch in chains[cc * HEADS:(cc + 1) * HEADS]:
            h, rs, sl = ch["h"], ch["rs"], ch["sl"]
            ch["o"] = ch["o_intra"] + _dot(ch["q16"], state[h].astype(BF16)) * xi_ref[h]
            state[h] = state[h] * gb_ref[h] + ch["kv"]
    for h in range(HEADS):
        st[h] = state[h]

    for ch in chains:
        ch["d"] = ch["o"] - jnp.mean(ch["o"], axis=-1, keepdims=True)
    for ch in chains:
        ch["var"] = jnp.mean(ch["d"] * ch["d"], axis=-1, keepdims=True)
    for ch in chains:
        rs, sl = ch["rs"], ch["sl"]
        o = ch["d"] * lax.rsqrt(ch["var"] + EPS) * rn_ref[:, sl]
        o_ref[rs, sl] = (o * _silu(z_ref[rs, sl])).astype(BF16)

    @pl.when(c == nc - 1)
    def _():
        sout_ref[...] = st[...]


def _ret(p3, cc, ss, consts, rn_row, s0, layer, depth, prev):
    b, t, _ = p3.shape
    has_state = s0 is not None
    nch = min(RET_CHUNKS_PER_STEP, t // CHUNK)
    rows = nch * CHUNK
    assert t % rows == 0
    dmat, xi, zeta, gblk = consts
    col = lambda blk: pl.BlockSpec((None, rows, WIDTH), lambda i, c: (i, c, blk))
    full = lambda shape: pl.BlockSpec(shape, lambda i, c: (0,) * len(shape))
    tab = pl.BlockSpec((rows, HEAD_DIM), lambda i, c: (c, 0))
    in_specs = [col(8), col(9), col(10), col(11), tab, tab,
                full((HEADS, CHUNK, CHUNK)), full((HEADS, CHUNK, HEAD_DIM)),
                full((HEADS, CHUNK, HEAD_DIM)), full((HEADS, 1, HEAD_DIM)), full((1, WIDTH))]
    args = [p3, p3, p3, p3, cc, ss, dmat, xi, zeta, gblk, rn_row]
    if has_state:
        in_specs.append(pl.BlockSpec((None, HEADS, HEAD_DIM, HEAD_DIM), lambda i, c: (i, 0, 0, 0)))
        args.append(s0)
    aliases = _stacked_out(prev, (1,), in_specs, args)
    return pl.pallas_call(
        functools.partial(_ret_kernel, has_state=has_state, nch=nch, n_alias=len(aliases)),
        grid=(b, t // rows),
        in_specs=in_specs,
        out_specs=[pl.BlockSpec((None, rows, WIDTH), lambda i, c: (i, c, 0)),
                   pl.BlockSpec((None, None, HEADS, HEAD_DIM, HEAD_DIM),
                                lambda i, c: (layer, i, 0, 0, 0))],
        out_shape=[jax.ShapeDtypeStruct((b, t, WIDTH), BF16),
                   jax.ShapeDtypeStruct((depth, b, HEADS, HEAD_DIM, HEAD_DIM), F32)],
        input_output_aliases=aliases,
        scratch_shapes=[pltpu.VMEM((HEADS, HEAD_DIM, HEAD_DIM), F32)],
        compiler_params=pltpu.CompilerParams(
            dimension_semantics=("arbitrary", "arbitrary"), vmem_limit_bytes=V7X_VMEM_LIMIT),
        name="ret",
    )(*args)


def _merge_kernel(oa_ref, ob_ref, oc_ref, ga_ref, gb_ref, gc_ref, wa_ref, wb_ref, wc_ref,
                  wo_ref, x_ref, g_ref, o_ref):
    merged = (jax.nn.sigmoid(ga_ref[...]) * _dot(oa_ref[...], wa_ref[...])
              + jax.nn.sigmoid(gb_ref[...]) * _dot(ob_ref[...], wb_ref[...])
              + jax.nn.sigmoid(gc_ref[...]) * _dot(oc_ref[...], wc_ref[...]))
    y = _dot(merged.astype(BF16), wo_ref[...])
    y = y * lax.rsqrt(jnp.mean(y * y, axis=-1, keepdims=True) + EPS) * g_ref[...]
    o_ref[...] = x_ref[...] + y


def _merge(oa, ob, oc, p2, x2d, wa, wb, wc, wo, g):
    m = x2d.shape[0]
    tm = min(m, MERGE_ROWS)
    assert m % tm == 0
    row = lambda w: pl.BlockSpec((tm, w), lambda i: (i, 0))
    gate = lambda blk: pl.BlockSpec((tm, D_MODEL), lambda i: (i, blk))
    const = lambda shape: pl.BlockSpec(shape, lambda i: (0, 0), pipeline_mode=pl.Buffered(1))
    return pl.pallas_call(
        _merge_kernel,
        grid=(m // tm,),
        in_specs=[row(WIDTH), row(WIDTH), row(WIDTH), gate(6), gate(7), gate(8),
                  const((WIDTH, D_MODEL)), const((WIDTH, D_MODEL)), const((WIDTH, D_MODEL)),
                  const((D_MODEL, D_MODEL)), row(D_MODEL), const((1, D_MODEL))],
        out_specs=row(D_MODEL),
        out_shape=jax.ShapeDtypeStruct((m, D_MODEL), F32),
        compiler_params=pltpu.CompilerParams(
            dimension_semantics=("arbitrary",), vmem_limit_bytes=V7X_VMEM_LIMIT),
        name="merge",
    )(oa, ob, oc, p2, p2, p2, wa, wb, wc, wo, x2d, g)


def _retention_constants():
    log_gamma = jnp.log1p(-jnp.exp2(-5.0 - jnp.arange(HEADS, dtype=F32)))
    idx = jnp.arange(CHUNK, dtype=F32)
    rel = idx[:, None] - idx[None, :]
    lg = log_gamma[:, None, None]
    dmat = jnp.where(rel >= 0, jnp.exp(lg * jnp.maximum(rel, 0.0)), 0.0)
    xi = jnp.exp(log_gamma[:, None] * (idx + 1.0))
    zeta = jnp.exp(log_gamma[:, None] * (CHUNK - 1.0 - idx))
    gblk = jnp.exp(log_gamma * CHUNK)
    bc = lambda v: jnp.broadcast_to(v[:, :, None], (HEADS, v.shape[1], HEAD_DIM))
    return dmat, bc(xi), bc(zeta), bc(gblk[:, None])


def _rotary_tables(pos):
    half = HEAD_DIM // 2
    inv = ROPE_BASE ** (-jnp.arange(half, dtype=F32) / half)
    ang = pos.astype(F32)[:, None] * inv[None, :]
    cos, sin = jnp.cos(ang), jnp.sin(ang)
    return jnp.concatenate([cos, cos], axis=1), jnp.concatenate([-sin, sin], axis=1)


def _rev_bias(rel_bias):
    n_off = BAND_KEYS + CHUNK - 1
    head = rel_bias[:, MAX_REL - (CHUNK - 1):].astype(F32)
    edge = jnp.broadcast_to(head[:, -1:], (rel_bias.shape[0], n_off - head.shape[1]))
    ext = jnp.concatenate([head, edge], axis=1)
    return jnp.pad(ext[:, ::-1], ((0, 0), (0, BIAS_LANES - n_off)))


def _lane_row(v, lane0):
    return jnp.zeros((1, GATE_LANES), F32).at[0, lane0:lane0 + v.shape[0]].set(v.astype(F32))


def _group_layer(x, pos, cache, conv_buf, s_delta, s_ret, lw, ret_consts, layer, depth, prev):
    (norm_pre, norm_post, w_main, w_gate, bias, conv_w, al_row, dt_row, dn_row, rn_row,
     wa, wb, wc, wo) = lw
    b, t, _ = x.shape
    x2d = x.reshape(b * t, D_MODEL)
    p2, gt2 = _inproj(x2d, norm_pre, w_main, w_gate)
    p3 = p2.reshape(b, t, N_MAIN)
    gt3 = gt2.reshape(b, t, GATE_LANES)
    prev_k, prev_v, prev_sd, prev_tail, prev_sr = prev

    oa, new_k, new_v = _attn(p3, bias, cache, layer, depth, (prev_k, prev_v))

    conv0 = None if conv_buf is None else jnp.pad(conv_buf, ((0, 0), (8 - (CONV_W - 1), 0), (0, 0)))
    ob, new_sd, conv_tail = _delta(p3, gt3, conv_w, al_row, dt_row, dn_row, conv0, s_delta,
                                   layer, depth, (prev_sd, prev_tail))

    cc, ss = _rotary_tables(pos)
    oc, new_sr = _ret(p3, cc, ss, ret_consts, rn_row, s_ret, layer, depth, (prev_sr,))

    y2d = _merge(oa.reshape(b * t, WIDTH), ob.reshape(b * t, WIDTH), oc.reshape(b * t, WIDTH),
                 p2, x2d, wa, wb, wc, wo, norm_post)
    return y2d.reshape(b, t, D_MODEL), (new_k, new_v, new_sd, conv_tail, new_sr)


def kernel(x_prompt, x_sample, cache_attn_k, cache_attn_v, state_conv, state_delta, state_ret,
           norm_pre, norm_post, w_in, attn_rel_bias, conv_w, delta_a_log, delta_dt_bias,
           delta_norm, ret_norm, w_branch_a, w_branch_b, w_branch_c, w_out):
    depth = w_in.shape[0]
    tp, ts = x_prompt.shape[1], x_sample.shape[1]
    pos_p = jnp.arange(tp, dtype=jnp.int32)
    pos_s = PAST_LEN + jnp.arange(ts, dtype=jnp.int32)
    ret_consts = _retention_constants()
    w_in_t = jnp.swapaxes(w_in, 1, 2)
    xp, xs = x_prompt, x_sample

    def empty_states(b):
        stacked = lambda *shape: jnp.zeros((depth, b) + shape, F32)
        return (stacked(BAND_ROWS, HEADS, HEAD_DIM), stacked(BAND_ROWS, HEADS, HEAD_DIM),
                stacked(HEADS, HEAD_DIM, HEAD_DIM), stacked(8, 3 * WIDTH),
                stacked(HEADS, HEAD_DIM, HEAD_DIM))

    st_p, st_s = empty_states(x_prompt.shape[0]), empty_states(x_sample.shape[0])
    for l in range(depth):
        w_main, w_gate = _weight_prep(w_in_t, l)
        lw = (norm_pre[l][None, :], norm_post[l][None, :], w_main, w_gate,
              _rev_bias(attn_rel_bias[l]), conv_w[l],
              _lane_row(delta_a_log[l], HEADS), _lane_row(delta_dt_bias[l], HEADS),
              delta_norm[l][None, :].astype(F32), ret_norm[l].reshape(1, WIDTH).astype(F32),
              w_branch_a[l].astype(BF16), w_branch_b[l].astype(BF16), w_branch_c[l].astype(BF16),
              w_out[l].astype(BF16))
        xp, st_p = _group_layer(xp, pos_p, None, None, None, None, lw, ret_consts,
                                l, depth, st_p)
        xs, st_s = _group_layer(xs, pos_s, (cache_attn_k, cache_attn_v), state_conv[l],
                                state_delta[l], state_ret[l], lw, ret_consts, l, depth, st_s)

    def finish(stacked):
        new_k, new_v, new_sd, conv_tail, new_sr = stacked
        return new_k, new_v, conv_tail[:, :, 8 - (CONV_W - 1):, :], new_sd, new_sr

    return (xp, xs, *finish(st_p), *finish(st_s))
```
